```python
import functools
import jax, jax.numpy as jnp
from jax import lax
import numpy as np

D_MODEL = 1024
BATCH = 16
SEQ = 2048
DEPTH = 1
DEC_BATCH = 32
DEC_SEQ = 8
PAST_LEN = 16384
PAGE_SIZE = 128

N_HEADS = 8
HEAD_DIM = 64
D_ATTN = N_HEADS * HEAD_DIM
D_CV = D_MODEL // 2
D_MIX = D_ATTN + D_CV
D_IN = 3 * D_ATTN + 2 * D_CV
MOBA_BLOCK = 256
MOBA_TOPK = 3
CV_WIDTH = 31
D_FF = 2816
FFN_CONV_WIDTH = 3
Q_CHUNK = 16
RMS_EPS = 1e-6
LN_EPS = 1e-5

kernel_name = 'hymba_moba_conformer_convffn_step'


def _rmsnorm(x, g):
    xf = x.astype(jnp.float32)
    y = xf * lax.rsqrt(jnp.mean(xf * xf, axis=-1, keepdims=True) + RMS_EPS)
    return (y * g.astype(jnp.float32)).astype(x.dtype)


def _layernorm(x, g, b):
    xf = x.astype(jnp.float32)
    mu = jnp.mean(xf, axis=-1, keepdims=True)
    var = jnp.mean(jnp.square(xf - mu), axis=-1, keepdims=True)
    y = (xf - mu) * lax.rsqrt(var + LN_EPS) * g.astype(jnp.float32) + b.astype(jnp.float32)
    return y.astype(x.dtype)


def _causal_dwconv(x_ext, w, b):
    W = w.shape[0]
    L = x_ext.shape[1] - W + 1
    y = x_ext[:, 0:L] * w[0]
    for j in range(1, W):
        y = y + x_ext[:, j:j + L] * w[j]
    return y + b


def _select_blocks(q, k_means, n_past, k_sel):
    s = jnp.einsum('bhqd,bhnd->bhqn', q, k_means).astype(jnp.float32)
    elig = jnp.arange(k_means.shape[2]) < n_past
    s = jnp.where(elig, s, -jnp.inf)
    _, idx = lax.top_k(s, k_sel)
    slot_ok = jnp.arange(k_sel) < jnp.minimum(n_past, k_sel)
    return idx, slot_ok


def _attend(q, k_sel, v_sel, sel_mask, k_own, v_own, own_mask):
    scale = HEAD_DIM ** -0.5
    lo = jnp.einsum('bhqd,bhkd->bhqk', q, k_own).astype(jnp.float32) * scale
    lo = jnp.where(own_mask, lo, -jnp.inf)
    if k_sel is None:
        p = jax.nn.softmax(lo, axis=-1).astype(v_own.dtype)
        return jnp.einsum('bhqk,bhkd->bhqd', p, v_own)
    ls = jnp.einsum('bhqd,bhqkd->bhqk', q, k_sel).astype(jnp.float32) * scale
    ls = jnp.where(sel_mask, ls, -jnp.inf)
    n_s = ls.shape[-1]
    p = jax.nn.softmax(jnp.concatenate([ls, lo], axis=-1), axis=-1).astype(v_own.dtype)
    return (jnp.einsum('bhqk,bhqkd->bhqd', p[..., :n_s], v_sel)
            + jnp.einsum('bhqk,bhkd->bhqd', p[..., n_s:], v_own))


def _moba_prompt(q, k, v):
    B, S, H, D = q.shape
    nb = -(-S // MOBA_BLOCK)
    pad = nb * MOBA_BLOCK - S
    qh = q.transpose(0, 2, 1, 3)
    def blocks(a):
        a = jnp.pad(a, ((0, 0), (0, pad), (0, 0), (0, 0)))
        return a.reshape(B, nb, MOBA_BLOCK, H, D).transpose(0, 3, 1, 2, 4)
    kb = blocks(k)
    vb = blocks(v)
    k_means = jnp.mean(kb.astype(jnp.float32), axis=3).astype(k.dtype)
    k_sel = min(MOBA_TOPK, nb)
    bi = jnp.arange(B)[:, None, None, None]
    hi = jnp.arange(H)[None, :, None, None]

    def chunk(c):
        start = c * Q_CHUNK
        qc = lax.dynamic_slice_in_dim(qh, start, Q_CHUNK, axis=2)
        blk = start // MOBA_BLOCK
        k_own = lax.dynamic_index_in_dim(kb, blk, axis=2, keepdims=False)
        v_own = lax.dynamic_index_in_dim(vb, blk, axis=2, keepdims=False)
        q_pos = start + jnp.arange(Q_CHUNK)
        k_pos = blk * MOBA_BLOCK + jnp.arange(MOBA_BLOCK)
        own_mask = k_pos[None, :] <= q_pos[:, None]
        idx, slot_ok = _select_blocks(qc, k_means, blk, k_sel)
        k_s = kb[bi, hi, idx].reshape(B, H, Q_CHUNK, k_sel * MOBA_BLOCK, D)
        v_s = vb[bi, hi, idx].reshape(B, H, Q_CHUNK, k_sel * MOBA_BLOCK, D)
        sel_mask = jnp.repeat(slot_ok, MOBA_BLOCK)
        return _attend(qc, k_s, v_s, sel_mask, k_own, v_own, own_mask)

    out = lax.map(chunk, jnp.arange(S // Q_CHUNK))
    return out.transpose(1, 0, 3, 2, 4).reshape(B, S, H * D)


def _moba_sample(q, k_new, v_new, *, pool_k, pool_v, page_table):
    DB, T, H, D = q.shape
    n_pages = page_table.shape[1]
    ppb = MOBA_BLOCK // PAGE_SIZE
    npb = (n_pages * PAGE_SIZE) // MOBA_BLOCK
    n_rem = n_pages - npb * ppb
    qh = q.transpose(0, 2, 1, 3)
    own_pt = page_table[:, npb * ppb:]
    k_own_c = pool_k[own_pt].reshape(DB, n_rem * PAGE_SIZE, H, D)
    v_own_c = pool_v[own_pt].reshape(DB, n_rem * PAGE_SIZE, H, D)
    k_own = jnp.concatenate([k_own_c, k_new], axis=1).transpose(0, 2, 1, 3)
    v_own = jnp.concatenate([v_own_c, v_new], axis=1).transpose(0, 2, 1, 3)
    own_mask = jnp.concatenate([jnp.ones((T, n_rem * PAGE_SIZE), bool),
                                jnp.tril(jnp.ones((T, T), bool))], axis=1)
    if npb == 0:
        out = _attend(qh, None, None, None, k_own, v_own, own_mask)
    else:
        k_sel = min(MOBA_TOPK, npb)
        kp = pool_k[page_table[:, :npb * ppb]]
        k_means = jnp.mean(kp.reshape(DB, npb, MOBA_BLOCK, H, D).astype(jnp.float32), axis=2)
        k_means = k_means.astype(q.dtype).transpose(0, 2, 1, 3)
        idx, slot_ok = _select_blocks(qh, k_means, npb, k_sel)
        logical = idx[..., None] * ppb + jnp.arange(ppb)
        phys = page_table[jnp.arange(DB)[:, None, None, None, None], logical]
        hi = jnp.arange(H)[None, :, None, None, None]
        k_s = pool_k[phys, :, hi, :].reshape(DB, H, T, k_sel * MOBA_BLOCK, D)
        v_s = pool_v[phys, :, hi, :].reshape(DB, H, T, k_sel * MOBA_BLOCK, D)
        sel_mask = jnp.repeat(slot_ok, MOBA_BLOCK)
        out = _attend(qh, k_s, v_s, sel_mask, k_own, v_own, own_mask)
    return out.transpose(0, 2, 1, 3).reshape(DB, T, H * D)


def _layer(x, moba_fn, cv_hist, ffn_hist, g_pre_mix, w_in, b_in, w_cv_dw, b_cv_dw, g_cv_ln, b_cv_ln,
           w_cv_pw, b_cv_pw, w_out, g_post_mix, g_pre_ffn, w_up, w_ffn_dw, b_ffn_dw, w_down, g_post_ffn):
    B, L, _ = x.shape
    h = _rmsnorm(x, g_pre_mix)
    z = h @ w_in + b_in
    q, k, v, cva, cvg = jnp.split(z, [D_ATTN, 2 * D_ATTN, 3 * D_ATTN, 3 * D_ATTN + D_CV], axis=-1)
    q = q.reshape(B, L, N_HEADS, HEAD_DIM)
    k = k.reshape(B, L, N_HEADS, HEAD_DIM)
    v = v.reshape(B, L, N_HEADS, HEAD_DIM)
    attn = moba_fn(q, k, v)
    glu = cva * jax.nn.sigmoid(cvg)
    glu_ext = jnp.concatenate([cv_hist, glu], axis=1)
    c = _causal_dwconv(glu_ext, w_cv_dw, b_cv_dw)
    c = jax.nn.silu(_layernorm(c, g_cv_ln, b_cv_ln))
    c = c @ w_cv_pw + b_cv_pw
    mix = jnp.concatenate([attn, c], axis=-1) @ w_out
    x = x + _rmsnorm(mix, g_post_mix)
    h = _rmsnorm(x, g_pre_ffn)
    u = h @ w_up
    u_ext = jnp.concatenate([ffn_hist, u], axis=1)
    u = _causal_dwconv(u_ext, w_ffn_dw, b_ffn_dw)
    gate, val = jnp.split(u, 2, axis=-1)
    f = (jax.nn.silu(gate) * val) @ w_down
    x = x + _rmsnorm(f, g_post_ffn)
    new_cv = glu_ext[:, -(CV_WIDTH - 1):]
    new_ffn = u_ext[:, -(FFN_CONV_WIDTH - 1):]
    return x, k, v, new_cv, new_ffn


def setup_inputs(seed: int = 0) -> dict:
    key = jax.random.key(seed)
    ks = jax.random.split(key, 24)
    f32 = jnp.float32
    n_pages = PAST_LEN // PAGE_SIZE
    n_phys = (DEC_BATCH * n_pages * 5 + 3) // 4
    def nrm(k, shape, s):
        return jax.random.normal(k, shape, f32) * s
    def gain(k, shape):
        return 1.0 + 0.02 * jax.random.normal(k, shape, f32)
    perm = jax.random.permutation(ks[2], n_phys)
    page_table = perm[:DEC_BATCH * n_pages].reshape(DEC_BATCH, n_pages).astype(jnp.int32)
    return {
        'x_prompt': nrm(ks[0], (BATCH, SEQ, D_MODEL), 1.0),
        'x_sample': nrm(ks[1], (DEC_BATCH, DEC_SEQ, D_MODEL), 1.0),
        'cache_k': nrm(ks[3], (DEPTH, n_phys, PAGE_SIZE, N_HEADS, HEAD_DIM), 1.0),
        'cache_v': nrm(ks[4], (DEPTH, n_phys, PAGE_SIZE, N_HEADS, HEAD_DIM), 1.0),
        'page_table': page_table,
        'state_cv': nrm(ks[5], (DEPTH, DEC_BATCH, CV_WIDTH - 1, D_CV), 0.5),
        'state_ffn': nrm(ks[6], (DEPTH, DEC_BATCH, FFN_CONV_WIDTH - 1, 2 * D_FF), 1.0),
        'g_pre_mix': gain(ks[7], (DEPTH, D_MODEL)),
        'w_in': nrm(ks[8], (DEPTH, D_MODEL, D_IN), D_MODEL ** -0.5),
        'b_in': nrm(ks[9], (DEPTH, D_IN), 0.02),
        'w_cv_dw': nrm(ks[10], (DEPTH, CV_WIDTH, D_CV), CV_WIDTH ** -0.5),
        'b_cv_dw': nrm(ks[11], (DEPTH, D_CV), 0.02),
        'g_cv_ln': gain(ks[12], (DEPTH, D_CV)),
        'b_cv_ln': nrm(ks[13], (DEPTH, D_CV), 0.02),
        'w_cv_pw': nrm(ks[14], (DEPTH, D_CV, D_CV), D_CV ** -0.5),
        'b_cv_pw': nrm(ks[15], (DEPTH, D_CV), 0.02),
        'w_out': nrm(ks[16], (DEPTH, D_MIX, D_MODEL), D_MIX ** -0.5),
        'g_post_mix': gain(ks[17], (DEPTH, D_MODEL)),
        'g_pre_ffn': gain(ks[18], (DEPTH, D_MODEL)),
        'w_up': nrm(ks[19], (DEPTH, D_MODEL, 2 * D_FF), D_MODEL ** -0.5),
        'w_ffn_dw': nrm(ks[20], (DEPTH, FFN_CONV_WIDTH, 2 * D_FF), FFN_CONV_WIDTH ** -0.5),
        'b_ffn_dw': nrm(ks[21], (DEPTH, 2 * D_FF), 0.02),
        'w_down': nrm(ks[22], (DEPTH, D_FF, D_MODEL), D_FF ** -0.5),
        'g_post_ffn': gain(ks[23], (DEPTH, D_MODEL)),
    }


def reference(x_prompt, x_sample, cache_k, cache_v, page_table, state_cv, state_ffn,
              g_pre_mix, w_in, b_in, w_cv_dw, b_cv_dw, g_cv_ln, b_cv_ln, w_cv_pw, b_cv_pw,
              w_out, g_post_mix, g_pre_ffn, w_up, w_ffn_dw, b_ffn_dw, w_down, g_post_ffn):
    yp = x_prompt
    ys = x_sample
    B = x_prompt.shape[0]
    kp_l, vp_l, cvp_l, ffp_l = [], [], [], []
    ks_l, vs_l, cvs_l, ffs_l = [], [], [], []
    for l in range(DEPTH):
        prm = (g_pre_mix[l], w_in[l], b_in[l], w_cv_dw[l], b_cv_dw[l], g_cv_ln[l], b_cv_ln[l],
               w_cv_pw[l], b_cv_pw[l], w_out[l], g_post_mix[l], g_pre_ffn[l], w_up[l],
               w_ffn_dw[l], b_ffn_dw[l], w_down[l], g_post_ffn[l])
        cv0 = jnp.zeros((B, CV_WIDTH - 1, D_CV), x_prompt.dtype)
        ff0 = jnp.zeros((B, FFN_CONV_WIDTH - 1, 2 * D_FF), x_prompt.dtype)
        yp, kp, vp, cvp, ffp = _layer(yp, _moba_prompt, cv0, ff0, *prm)
        moba_s = functools.partial(_moba_sample, pool_k=cache_k[l], pool_v=cache_v[l],
                                   page_table=page_table)
        ys, ksm, vsm, cvs, ffs = _layer(ys, moba_s, state_cv[l], state_ffn[l], *prm)
        kp_l.append(kp); vp_l.append(vp); cvp_l.append(cvp); ffp_l.append(ffp)
        ks_l.append(ksm); vs_l.append(vsm); cvs_l.append(cvs); ffs_l.append(ffs)
    return (yp, ys, jnp.stack(kp_l), jnp.stack(vp_l), jnp.stack(cvp_l), jnp.stack(ffp_l),
            jnp.stack(ks_l), jnp.stack(vs_l), jnp.stack(cvs_l), jnp.stack(ffs_l))
```

```python
import functools

import jax
import jax.numpy as jnp
from jax import lax
from jax.experimental import pallas as pl
from jax.experimental.pallas import tpu as pltpu

N_HEADS = 8
HEAD_DIM = 64
D_ATTN = N_HEADS * HEAD_DIM
MOBA_BLOCK = 256
MOBA_TOPK = 3
RMS_EPS = 1e-6
LN_EPS = 1e-5
NEG = -1e30

LANES = 128
SUBLANES = 8
VMEM_LIMIT = 56 * 1024 * 1024

ROW_TILE = 512
CONV_CHUNK = 32
FF_CHUNK = 256
PAGES_PER_STEP = 8

_NT = (((1,), (1,)), ((), ()))


def _const_spec(shape):
    nd = len(shape)
    return pl.BlockSpec(shape, lambda *_: (0,) * nd, pipeline_mode=pl.Buffered(1))


def _rms(x, g):
    return x * lax.rsqrt(jnp.mean(x * x, axis=-1, keepdims=True) + RMS_EPS) * g


def _silu(x):
    return x / (1.0 + jnp.exp(-x))


def _inproj_conv_kernel(x_ref, hist_ref, gpre_ref, win_ref, bin_ref, wdw_ref, bdw_ref,
                        gln_ref, bln_ref, wpw_ref, bpw_ref,
                        q_ref, k_ref, v_ref, c_ref, cvs_ref,
                        ext_ref, act_ref, *, gt, l, width):
    s = pl.program_id(1)
    rows = gt * l
    dc = act_ref.shape[-1]
    pad = ext_ref.shape[1] - l
    h0 = pad - (width - 1)

    @pl.when(s == 0)
    def _():
        ext_ref[:, h0:pad, :] = hist_ref[...]

    h = _rms(x_ref[...], gpre_ref[...]).astype(jnp.bfloat16)

    def proj(lo, hi):
        return jnp.dot(h, win_ref[:, lo:hi], preferred_element_type=jnp.float32) + bin_ref[:, lo:hi]

    q_ref[...] = proj(0, D_ATTN)
    k_ref[...] = proj(D_ATTN, 2 * D_ATTN)
    v_ref[...] = proj(2 * D_ATTN, 3 * D_ATTN)
    cva = proj(3 * D_ATTN, 3 * D_ATTN + dc)
    cvg = proj(3 * D_ATTN + dc, 3 * D_ATTN + 2 * dc)
    glu = cva / (1.0 + jnp.exp(-cvg))
    ext_ref[:, pad:pad + l, :] = glu.reshape(gt, l, dc)

    rc = min(l, CONV_CHUNK)
    n_chunks = l // rc

    def conv_chunk(it, carry):
        g = it // n_chunks
        r0 = pl.multiple_of((it % n_chunks) * rc, SUBLANES)
        win = ext_ref[g, pl.ds(r0, rc + pad), :]
        acc = win[h0:h0 + rc] * wdw_ref[0:1, :]
        for j in range(1, width):
            acc = acc + win[h0 + j:h0 + j + rc] * wdw_ref[j:j + 1, :]
        acc = acc + bdw_ref[...]
        mu = jnp.mean(acc, axis=-1, keepdims=True)
        cen = acc - mu
        var = jnp.mean(cen * cen, axis=-1, keepdims=True)
        y = cen * lax.rsqrt(var + LN_EPS) * gln_ref[...] + bln_ref[...]
        act_ref[pl.ds(pl.multiple_of(g * l + r0, SUBLANES), rc), :] = _silu(y)
        return carry

    lax.fori_loop(0, gt * n_chunks, conv_chunk, 0)

    c = jnp.dot(act_ref[...].astype(jnp.bfloat16), wpw_ref[...],
                preferred_element_type=jnp.float32) + bpw_ref[...]
    c_ref[...] = c.astype(c_ref.dtype)

    @pl.when(s == pl.num_programs(1) - 1)
    def _():
        cvs_ref[...] = ext_ref[:, l + h0:l + pad, :]

    ext_ref[:, 0:pad, :] = ext_ref[:, l:l + pad, :]


def _inproj_conv(x, hist, gpre, win, b_in, wdw, bdw, gln, bln, wpw, bpw, *, gt, l):
    g_total, s_total, d = x.shape
    n = g_total * s_total
    dc = wpw.shape[0]
    width = wdw.shape[0]
    pad = -(-(width - 1) // SUBLANES) * SUBLANES
    rows = gt * l
    n_s = s_total // l
    assert s_total % l == 0 and g_total % gt == 0 and l % SUBLANES == 0
    assert gt == 1 or n_s == 1
    row_spec = lambda c: pl.BlockSpec((rows, c), lambda g, s: (g * n_s + s, 0))
    f32 = jnp.float32
    outs = pl.pallas_call(
        functools.partial(_inproj_conv_kernel, gt=gt, l=l, width=width),
        grid=(g_total // gt, n_s),
        in_specs=[
            row_spec(d),
            pl.BlockSpec((gt, width - 1, dc), lambda g, s: (g, 0, 0)),
            _const_spec((1, d)), _const_spec(win.shape), _const_spec((1, win.shape[1])),
            _const_spec(wdw.shape), _const_spec((1, dc)), _const_spec((1, dc)), _const_spec((1, dc)),
            _const_spec(wpw.shape), _const_spec((1, dc)),
        ],
        out_specs=[
            row_spec(D_ATTN), row_spec(D_ATTN), row_spec(D_ATTN), row_spec(dc),
            pl.BlockSpec((gt, width - 1, dc), lambda g, s: (g, 0, 0)),
        ],
        out_shape=[
            jax.ShapeDtypeStruct((n, D_ATTN), f32), jax.ShapeDtypeStruct((n, D_ATTN), f32),
            jax.ShapeDtypeStruct((n, D_ATTN), f32), jax.ShapeDtypeStruct((n, dc), jnp.bfloat16),
            jax.ShapeDtypeStruct((g_total, width - 1, dc), f32),
        ],
        scratch_shapes=[pltpu.VMEM((gt, pad + l, dc), f32), pltpu.VMEM((rows, dc), f32)],
        compiler_params=pltpu.CompilerParams(
            dimension_semantics=("arbitrary", "arbitrary"), vmem_limit_bytes=VMEM_LIMIT),
        name="inproj_conv",
    )(x.reshape(n, d), hist, gpre.reshape(1, d), win, b_in.reshape(1, -1), wdw, bdw.reshape(1, dc),
      gln.reshape(1, dc), bln.reshape(1, dc), wpw, bpw.reshape(1, dc))
    return outs


def _moba_prompt_kernel(q_ref, k_ref, v_ref, o_ref, kaug_ref, vb_ref, km_ref, *, nb):
    blk = MOBA_BLOCK
    s_len = q_ref.shape[1]
    lane = lax.broadcasted_iota(jnp.int32, (blk, LANES), 1)
    scale = HEAD_DIM ** -0.5

    kp = k_ref[0]
    vb_ref[...] = v_ref[0].astype(jnp.bfloat16)
    km = jnp.mean(kp.reshape(nb, blk, LANES), axis=1)
    row_blk = lax.broadcasted_iota(jnp.int32, (s_len, LANES), 0) // blk
    lane_s = lax.broadcasted_iota(jnp.int32, (s_len, LANES), 1)
    for hh in range(2):
        own_lo = hh * HEAD_DIM
        base = HEAD_DIM - own_lo
        own_s = (lane_s >= own_lo) & (lane_s < own_lo + HEAD_DIM)
        onehot = (lane_s - base == row_blk).astype(jnp.float32)
        kaug_ref[hh] = jnp.where(own_s, kp, onehot).astype(jnp.bfloat16)
        own_m = (lax.broadcasted_iota(jnp.int32, (nb, LANES), 1) >= own_lo) & \
                (lax.broadcasted_iota(jnp.int32, (nb, LANES), 1) < own_lo + HEAD_DIM)
        km_h = jnp.where(own_m, km, 0.0)
        km_ref[hh] = jnp.zeros((LANES, LANES), jnp.float32)
        km_ref[hh, base:base + nb, :] = km_h

    rr = lax.broadcasted_iota(jnp.int32, (blk, blk), 0)
    cc = lax.broadcasted_iota(jnp.int32, (blk, blk), 1)
    causal = cc <= rr

    def q_block(i, carry):
        r0 = pl.multiple_of(i * blk, blk)
        qp = q_ref[0, pl.ds(r0, blk), :]
        outs = []
        for hh in range(2):
            own_lo = hh * HEAD_DIM
            base = HEAD_DIM - own_lo
            own = (lane >= own_lo) & (lane < own_lo + HEAD_DIM)
            nbl = lane - base
            sc = lax.dot_general(jnp.where(own, qp, 0.0), km_ref[hh], _NT,
                                 precision=lax.Precision.HIGHEST,
                                 preferred_element_type=jnp.float32)
            cnt = jnp.zeros((blk, LANES), jnp.int32)
            for m in range(nb - 1):
                col = sc[:, base + m:base + m + 1]
                beats = (col > sc) | ((col == sc) & (m < nbl))
                cnt = cnt + jnp.where(beats, (m < i).astype(jnp.int32), 0)
            keep = ((nbl < i) & (cnt < MOBA_TOPK)) | (nbl == i)
            bias = jnp.where(keep, 0.0, NEG)
            in_bias = (nbl >= 0) & (nbl < nb)
            q_aug = jnp.where(own, qp * scale, jnp.where(in_bias, bias, 0.0)).astype(jnp.bfloat16)

            kd = kaug_ref[hh, pl.ds(r0, blk), :]
            sd = lax.dot_general(q_aug, kd, _NT, preferred_element_type=jnp.float32)
            sd = jnp.where(causal, sd, NEG)
            m0 = jnp.max(sd, axis=-1, keepdims=True)
            p0 = jnp.exp(sd - m0)
            l0 = jnp.sum(p0, axis=-1, keepdims=True)
            a0 = jnp.dot(p0.astype(jnp.bfloat16), vb_ref[pl.ds(r0, blk), :],
                         preferred_element_type=jnp.float32)

            def kv_block(j, st):
                m_run, l_run, acc = st
                c0 = pl.multiple_of(j * blk, blk)
                sj = lax.dot_general(q_aug, kaug_ref[hh, pl.ds(c0, blk), :], _NT,
                                     preferred_element_type=jnp.float32)
                m_new = jnp.maximum(m_run, jnp.max(sj, axis=-1, keepdims=True))
                alpha = jnp.exp(m_run - m_new)
                pj = jnp.exp(sj - m_new)
                l_new = alpha * l_run + jnp.sum(pj, axis=-1, keepdims=True)
                acc_new = alpha * acc + jnp.dot(pj.astype(jnp.bfloat16), vb_ref[pl.ds(c0, blk), :],
                                                preferred_element_type=jnp.float32)
                return m_new, l_new, acc_new

            _, l_fin, acc = lax.fori_loop(0, i, kv_block, (m0, l0, a0))
            outs.append(acc / l_fin)
        o_ref[0, pl.ds(r0, blk), :] = jnp.where(lane < HEAD_DIM, outs[0], outs[1]).astype(o_ref.dtype)
        return carry

    lax.fori_loop(0, nb, q_block, 0)


def _moba_prompt(q, k, v):
    b, s_len, da = q.shape
    assert s_len % MOBA_BLOCK == 0 and da % LANES == 0 and LANES == 2 * HEAD_DIM
    nb = s_len // MOBA_BLOCK
    assert nb <= HEAD_DIM
    spec = pl.BlockSpec((1, s_len, LANES), lambda bi, hp: (bi, 0, hp))
    return pl.pallas_call(
        functools.partial(_moba_prompt_kernel, nb=nb),
        grid=(b, da // LANES),
        in_specs=[spec, spec, spec],
        out_specs=spec,
        out_shape=jax.ShapeDtypeStruct((b, s_len, da), jnp.bfloat16),
        scratch_shapes=[pltpu.VMEM((2, s_len, LANES), jnp.bfloat16),
                        pltpu.VMEM((s_len, LANES), jnp.bfloat16),
                        pltpu.VMEM((2, LANES, LANES), jnp.float32)],
        compiler_params=pltpu.CompilerParams(
            dimension_semantics=("arbitrary", "arbitrary"), vmem_limit_bytes=VMEM_LIMIT),
        name="moba_prompt",
    )(q, k, v)


def _fold_heads(o, row_head):
    r, da = o.shape
    lane_head = lax.broadcasted_iota(jnp.int32, (r, da), 1) // HEAD_DIM
    om = jnp.where(lane_head == row_head, o, 0.0)
    out = om[:, 0:LANES]
    for p in range(1, da // LANES):
        out = out + om[:, p * LANES:(p + 1) * LANES]
    return out


def _moba_sample_kernel(pt_ref, q_ref, kn_ref, vn_ref, *rest, t, npb, ppb, pps):
    kp_refs = rest[0:pps]
    vp_refs = rest[pps:2 * pps]
    o_ref = rest[2 * pps]
    km_ref, m_ref, l_ref, op_ref = rest[2 * pps + 1:]
    del pt_ref
    step = pl.program_id(1)
    r = N_HEADS * t
    da = q_ref.shape[-1]
    bps = pps // ppb
    scale = HEAD_DIM ** -0.5

    q = q_ref[...]
    q_rep = jnp.concatenate([q] * N_HEADS, axis=0)
    row_head = lax.broadcasted_iota(jnp.int32, (r, da), 0) // t
    lane_head = lax.broadcasted_iota(jnp.int32, (r, da), 1) // HEAD_DIM
    q_exp = jnp.where(row_head == lane_head, q_rep, 0.0)
    q_log = (q_exp * scale).astype(jnp.bfloat16)
    lane = lax.broadcasted_iota(jnp.int32, (r, LANES), 1)

    @pl.when(step == 0)
    def _():
        km_ref[...] = jnp.zeros(km_ref.shape, km_ref.dtype)
        m_ref[...] = jnp.full(m_ref.shape, NEG, m_ref.dtype)
        l_ref[...] = jnp.zeros(l_ref.shape, l_ref.dtype)

    for bb in range(bps):
        n = step * bps + bb
        kb = jnp.concatenate([kp_refs[bb * ppb + p][0] for p in range(ppb)], axis=0)
        vb = jnp.concatenate([vp_refs[bb * ppb + p][0] for p in range(ppb)], axis=0)
        km_ref[pl.ds(n, 1), :] = jnp.mean(kb, axis=0, keepdims=True)
        sb = lax.dot_general(q_log, kb.astype(jnp.bfloat16), _NT, preferred_element_type=jnp.float32)
        mb = jnp.max(sb, axis=-1, keepdims=True)
        pb = jnp.exp(sb - mb)
        lb = jnp.sum(pb, axis=-1, keepdims=True)
        ob = jnp.dot(pb.astype(jnp.bfloat16), vb.astype(jnp.bfloat16), preferred_element_type=jnp.float32)
        op_ref[n] = _fold_heads(ob, row_head)
        m_ref[...] = jnp.where(lane == n, mb, m_ref[...])
        l_ref[...] = jnp.where(lane == n, lb, l_ref[...])

    @pl.when(step == pl.num_programs(1) - 1)
    def _():
        sc = lax.dot_general(q_exp, km_ref[...], _NT, precision=lax.Precision.HIGHEST,
                             preferred_element_type=jnp.float32)
        sc = jnp.where(lane < npb, sc, -jnp.inf)
        sel = jnp.zeros((r, LANES), jnp.bool_)
        for _ in range(min(MOBA_TOPK, npb)):
            best = jnp.max(sc, axis=-1, keepdims=True)
            first = jnp.min(jnp.where(sc == best, lane, LANES), axis=-1, keepdims=True)
            hit = lane == first
            sel = sel | hit
            sc = jnp.where(hit, -jnp.inf, sc)

        kn = kn_ref[...]
        vn = vn_ref[...]
        so = lax.dot_general(q_log, kn.astype(jnp.bfloat16), _NT, preferred_element_type=jnp.float32)
        tq = lax.broadcasted_iota(jnp.int32, (r, t), 0) % t
        tk = lax.broadcasted_iota(jnp.int32, (r, t), 1)
        so = jnp.where(tk <= tq, so, NEG)
        m_own = jnp.max(so, axis=-1, keepdims=True)
        p_own = jnp.exp(so - m_own)
        l_own = jnp.sum(p_own, axis=-1, keepdims=True)
        o_own = _fold_heads(jnp.dot(p_own.astype(jnp.bfloat16), vn.astype(jnp.bfloat16),
                                    preferred_element_type=jnp.float32), row_head)

        m_sel = jnp.where(sel, m_ref[...], NEG)
        m_tot = jnp.maximum(m_own, jnp.max(m_sel, axis=-1, keepdims=True))
        w = jnp.where(sel, jnp.exp(m_sel - m_tot), 0.0)
        w_own = jnp.exp(m_own - m_tot)
        l_tot = w_own * l_own + jnp.sum(w * l_ref[...], axis=-1, keepdims=True)

        def add_block(nn, acc):
            wn = jnp.sum(jnp.where(lane == nn, w, 0.0), axis=-1, keepdims=True)
            return acc + wn * op_ref[nn]

        o_tot = lax.fori_loop(0, npb, add_block, w_own * o_own) / l_tot
        tiles = []
        for p in range(da // LANES):
            lo = o_tot[(2 * p) * t:(2 * p + 1) * t, :]
            hi = o_tot[(2 * p + 1) * t:(2 * p + 2) * t, :]
            tiles.append(jnp.where(lane[0:t] < HEAD_DIM, lo, hi))
        o_ref[...] = jnp.concatenate(tiles, axis=-1).astype(o_ref.dtype)


def _moba_sample(q, k_new, v_new, pool_k, pool_v, page_table, *, t):
    n, da = q.shape
    db, n_pages = page_table.shape
    n_phys, page = pool_k.shape[0], pool_k.shape[1]
    ppb = MOBA_BLOCK // page
    npb = (n_pages * page) // MOBA_BLOCK
    assert MOBA_BLOCK % page == 0 and n_pages == npb * ppb and npb >= 1 and npb <= LANES
    pps = PAGES_PER_STEP
    assert pps % ppb == 0 and n_pages % pps == 0 and t % SUBLANES == 0
    pk = pool_k.reshape(n_phys, page, da)
    pv = pool_v.reshape(n_phys, page, da)
    r = N_HEADS * t

    def page_spec(j):
        return pl.BlockSpec((1, page, da), lambda d, s, pt: (pt[d, s * pps + j], 0, 0))

    row_spec = pl.BlockSpec((t, da), lambda d, s, pt: (d, 0))
    grid_spec = pltpu.PrefetchScalarGridSpec(
        num_scalar_prefetch=1,
        grid=(db, n_pages // pps),
        in_specs=[row_spec, row_spec, row_spec] + [page_spec(j) for j in range(pps)] * 2,
        out_specs=row_spec,
        scratch_shapes=[pltpu.VMEM((LANES, da), jnp.float32),
                        pltpu.VMEM((r, LANES), jnp.float32),
                        pltpu.VMEM((r, LANES), jnp.float32),
                        pltpu.VMEM((npb, r, LANES), jnp.float32)],
    )
    return pl.pallas_call(
        functools.partial(_moba_sample_kernel, t=t, npb=npb, ppb=ppb, pps=pps),
        grid_spec=grid_spec,
        out_shape=jax.ShapeDtypeStruct((n, da), jnp.bfloat16),
        compiler_params=pltpu.CompilerParams(
            dimension_semantics=("arbitrary", "arbitrary"), vmem_limit_bytes=VMEM_LIMIT),
        name="moba_sample",
    )(page_table, q, k_new, v_new, *([pk] * pps), *([pv] * pps))


def _out_ffn_kernel(x_ref, a_ref, c_ref, hist_ref, wo_ref, gpm_ref, gpf_ref, wup_ref, wdw_ref, bdw_ref,
                    wdn_ref, gpo_ref, y_ref, st_ref, carry_ref, ug_ref, uv_ref, f_ref, *, gt, l, width):
    s = pl.program_id(1)
    d_ff = wdn_ref.shape[0]
    da = a_ref.shape[-1]
    pad = carry_ref.shape[1]
    h0 = pad - (width - 1)

    @pl.when(s == 0)
    def _():
        carry_ref[:, h0:pad, :] = hist_ref[...]

    mix = (jnp.dot(a_ref[...], wo_ref[0:da, :], preferred_element_type=jnp.float32)
           + jnp.dot(c_ref[...], wo_ref[da:, :], preferred_element_type=jnp.float32))
    x1 = x_ref[...] + _rms(mix, gpm_ref[...])
    h = _rms(x1, gpf_ref[...]).astype(jnp.bfloat16)

    def up_conv(buf, lo):
        hi = lo + FF_CHUNK
        buf[:, 0:pad, :] = carry_ref[:, :, lo:hi]
        u = jnp.dot(h, wup_ref[:, lo:hi], preferred_element_type=jnp.float32)
        buf[:, pad:pad + l, :] = u.reshape(gt, l, FF_CHUNK)
        carry_ref[:, :, lo:hi] = buf[:, l:l + pad, :]
        acc = buf[:, h0:h0 + l, :] * wdw_ref[0:1, lo:hi]
        for j in range(1, width):
            acc = acc + buf[:, h0 + j:h0 + j + l, :] * wdw_ref[j:j + 1, lo:hi]
        return (acc + bdw_ref[:, lo:hi]).reshape(gt * l, FF_CHUNK)

    for cc in range(d_ff // FF_CHUNK):
        gate = up_conv(ug_ref, cc * FF_CHUNK)
        val = up_conv(uv_ref, d_ff + cc * FF_CHUNK)
        f_ref[:, cc * FF_CHUNK:(cc + 1) * FF_CHUNK] = (_silu(gate) * val).astype(jnp.bfloat16)

    ff = jnp.dot(f_ref[...], wdn_ref[...], preferred_element_type=jnp.float32)
    y_ref[...] = x1 + _rms(ff, gpo_ref[...])

    @pl.when(s == pl.num_programs(1) - 1)
    def _():
        st_ref[...] = carry_ref[:, h0:pad, :]


def _out_ffn(x, attn, c, hist, wo, gpm, gpf, wup, wdw, bdw, wdn, gpo, *, gt, l):
    g_total, s_total, d = x.shape
    n = g_total * s_total
    d_ff = wdn.shape[0]
    width = wdw.shape[0]
    pad = SUBLANES
    assert width - 1 <= pad <= l and d_ff % FF_CHUNK == 0
    rows = gt * l
    n_s = s_total // l
    assert s_total % l == 0 and g_total % gt == 0 and l % SUBLANES == 0
    assert gt == 1 or n_s == 1
    row_spec = lambda cdim: pl.BlockSpec((rows, cdim), lambda g, s: (g * n_s + s, 0))
    st_spec = pl.BlockSpec((gt, width - 1, 2 * d_ff), lambda g, s: (g, 0, 0))
    f32 = jnp.float32
    return pl.pallas_call(
        functools.partial(_out_ffn_kernel, gt=gt, l=l, width=width),
        grid=(g_total // gt, n_s),
        in_specs=[
            row_spec(d), row_spec(attn.shape[1]), row_spec(c.shape[1]), st_spec,
            _const_spec(wo.shape), _const_spec((1, d)), _const_spec((1, d)),
            _const_spec(wup.shape), _const_spec(wdw.shape), _const_spec((1, 2 * d_ff)),
            _const_spec(wdn.shape), _const_spec((1, d)),
        ],
        out_specs=[row_spec(d), st_spec],
        out_shape=[jax.ShapeDtypeStruct((n, d), f32),
                   jax.ShapeDtypeStruct((g_total, width - 1, 2 * d_ff), f32)],
        scratch_shapes=[pltpu.VMEM((gt, pad, 2 * d_ff), f32),
                        pltpu.VMEM((gt, pad + l, FF_CHUNK), f32),
                        pltpu.VMEM((gt, pad + l, FF_CHUNK), f32),
                        pltpu.VMEM((rows, d_ff), jnp.bfloat16)],
        compiler_params=pltpu.CompilerParams(
            dimension_semantics=("arbitrary", "arbitrary"), vmem_limit_bytes=VMEM_LIMIT),
        name="out_ffn",
    )(x.reshape(n, d), attn, c, hist, wo, gpm.reshape(1, d), gpf.reshape(1, d), wup, wdw,
      bdw.reshape(1, -1), wdn, gpo.reshape(1, d))


def _layer(x, cv_hist, ffn_hist, prm, moba_fn, *, gt, l):
    (gpre, win, b_in, wdw_cv, bdw_cv, gln, bln, wpw, bpw, wo, gpm, gpf, wup, wdw_ff, bdw_ff, wdn, gpo) = prm
    g_total, s_total, d = x.shape
    q, k, v, c, cv_new = _inproj_conv(x, cv_hist, gpre, win, b_in, wdw_cv, bdw_cv, gln, bln, wpw, bpw,
                                      gt=gt, l=l)
    attn = moba_fn(q, k, v)
    y, ffn_new = _out_ffn(x, attn, c, ffn_hist, wo, gpm, gpf, wup, wdw_ff, bdw_ff, wdn, gpo, gt=gt, l=l)
    kv_shape = (g_total, s_total, N_HEADS, HEAD_DIM)
    return y.reshape(x.shape), k.reshape(kv_shape), v.reshape(kv_shape), cv_new, ffn_new


def kernel(x_prompt, x_sample, cache_k, cache_v, page_table, state_cv, state_ffn, g_pre_mix, w_in, b_in,
           w_cv_dw, b_cv_dw, g_cv_ln, b_cv_ln, w_cv_pw, b_cv_pw, w_out, g_post_mix, g_pre_ffn, w_up,
           w_ffn_dw, b_ffn_dw, w_down, g_post_ffn):
    depth = w_in.shape[0]
    bf16 = jnp.bfloat16
    b, s_len, d = x_prompt.shape
    db, t, _ = x_sample.shape
    yp, ys = x_prompt, x_sample
    outs = [[] for _ in range(8)]
    for lyr in range(depth):
        prm = (g_pre_mix[lyr], w_in[lyr].astype(bf16), b_in[lyr], w_cv_dw[lyr], b_cv_dw[lyr], g_cv_ln[lyr],
               b_cv_ln[lyr], w_cv_pw[lyr].astype(bf16), b_cv_pw[lyr], w_out[lyr].astype(bf16), g_post_mix[lyr],
               g_pre_ffn[lyr], w_up[lyr].astype(bf16), w_ffn_dw[lyr], b_ffn_dw[lyr], w_down[lyr].astype(bf16),
               g_post_ffn[lyr])
        cv0 = jnp.zeros((b,) + state_cv.shape[2:], x_prompt.dtype)
        ff0 = jnp.zeros((b,) + state_ffn.shape[2:], x_prompt.dtype)

        def moba_p(q, k, v):
            da = q.shape[-1]
            o = _moba_prompt(q.reshape(b, s_len, da), k.reshape(b, s_len, da), v.reshape(b, s_len, da))
            return o.reshape(b * s_len, da)

        yp, kp, vp, cvp, ffp = _layer(yp, cv0, ff0, prm, moba_p, gt=1, l=min(ROW_TILE, s_len))
        moba_s = functools.partial(_moba_sample, pool_k=cache_k[lyr], pool_v=cache_v[lyr],
                                   page_table=page_table, t=t)
        ys, ks, vs, cvs, ffs = _layer(ys, state_cv[lyr], state_ffn[lyr], prm, moba_s, gt=db, l=t)
        for lst, val in zip(outs, (kp, vp, cvp, ffp, ks, vs, cvs, ffs)):
            lst.append(val)
    return (yp, ys) + tuple(jnp.stack(lst) for lst in outs)
```

```python
import functools

import jax
import jax.numpy as jnp
from jax import lax
from jax.experimental import pallas as pl
from jax.experimental.pallas import tpu as pltpu

N_HEADS = 8
HEAD_DIM = 64
D_ATTN = N_HEADS * HEAD_DIM
MOBA_BLOCK = 256
MOBA_TOPK = 3
RMS_EPS = 1e-6
LN_EPS = 1e-5
NEG = -1e30
LOG2E = 1.4426950408889634

LANES = 128
SUBLANES = 8
VMEM_LIMIT = 56 * 1024 * 1024

ROW_TILE = 512
CONV_CHUNK = 32
FF_CHUNK = 256
PAGES_PER_STEP = 16

_NT = (((1,), (1,)), ((), ()))
_HI = lax.Precision.HIGHEST


def _const_spec(shape):
    nd = len(shape)
    return pl.BlockSpec(shape, lambda *_: (0,) * nd, pipeline_mode=pl.Buffered(1))


def _rms(x, g):
    return x * lax.rsqrt(jnp.mean(x * x, axis=-1, keepdims=True) + RMS_EPS) * g


def _silu(x):
    return x / (1.0 + jnp.exp(-x))


def _inproj_conv_kernel(x_ref, hist_ref, gpre_ref, wqkv_ref, bqkv_ref, wcv_ref, bcv_ref, wdw_ref, bdw_ref,
                        gln_ref, bln_ref, wpw_ref, bpw_ref,
                        q_ref, k_ref, v_ref, c_ref, cvs_ref,
                        ext_ref, act_ref, shf_ref, *, gt, l, width, transposed):
    s = pl.program_id(1)
    dc = act_ref.shape[-1]
    pad = ext_ref.shape[1] - l
    h0 = pad - (width - 1)

    @pl.when(s == 0)
    def _():
        ext_ref[:, h0:pad, :] = hist_ref[...]

    h = _rms(x_ref[...], gpre_ref[...]).astype(jnp.bfloat16)

    for idx, ref in enumerate((q_ref, k_ref, v_ref)):
        lo, hi = idx * D_ATTN, (idx + 1) * D_ATTN
        if transposed:
            ref[0] = lax.dot_general(wqkv_ref[lo:hi, :], h, _NT,
                                     preferred_element_type=jnp.float32) + bqkv_ref[lo:hi, :]
        else:
            ref[...] = jnp.dot(h, wqkv_ref[:, lo:hi], preferred_element_type=jnp.float32) + bqkv_ref[:, lo:hi]

    cva = jnp.dot(h, wcv_ref[:, 0:dc], preferred_element_type=jnp.float32) + bcv_ref[:, 0:dc]
    cvg = jnp.dot(h, wcv_ref[:, dc:2 * dc], preferred_element_type=jnp.float32) + bcv_ref[:, dc:2 * dc]
    glu = cva / (1.0 + jnp.exp(-cvg))
    ext_ref[:, pad:pad + l, :] = glu.reshape(gt, l, dc)

    rc = min(l, CONV_CHUNK)
    n_chunks = l // rc

    def conv_chunk(it, carry):
        g = it // n_chunks
        r0 = pl.multiple_of((it % n_chunks) * rc, SUBLANES)
        win = ext_ref[g, pl.ds(r0, rc + pad), :]
        for r in range(1, SUBLANES):
            shf_ref[r - 1] = win[r:r + rc + pad - SUBLANES]
        acc = None
        for j in range(width):
            a, r = divmod(h0 + j, SUBLANES)
            lo = a * SUBLANES
            tap = win[lo:lo + rc] if r == 0 else shf_ref[r - 1, lo:lo + rc, :]
            term = tap * wdw_ref[j:j + 1, :]
            acc = term if acc is None else acc + term
        acc = acc + bdw_ref[...]
        mu = jnp.mean(acc, axis=-1, keepdims=True)
        cen = acc - mu
        var = jnp.mean(cen * cen, axis=-1, keepdims=True)
        y = cen * lax.rsqrt(var + LN_EPS) * gln_ref[...] + bln_ref[...]
        act_ref[pl.ds(pl.multiple_of(g * l + r0, SUBLANES), rc), :] = _silu(y)
        return carry

    lax.fori_loop(0, gt * n_chunks, conv_chunk, 0)

    c = jnp.dot(act_ref[...].astype(jnp.bfloat16), wpw_ref[...],
                preferred_element_type=jnp.float32) + bpw_ref[...]
    c_ref[...] = c.astype(c_ref.dtype)

    @pl.when(s == pl.num_programs(1) - 1)
    def _():
        cvs_ref[...] = ext_ref[:, l + h0:l + pad, :]

    ext_ref[:, 0:pad, :] = ext_ref[:, l:l + pad, :]


def _inproj_conv(x, hist, gpre, wqkv, bqkv, wcv, bcv, wdw, bdw, gln, bln, wpw, bpw, *, gt, l, transposed):
    g_total, s_total, d = x.shape
    n = g_total * s_total
    dc = wpw.shape[0]
    width = wdw.shape[0]
    pad = -(-(width - 1) // SUBLANES) * SUBLANES
    rows = gt * l
    n_s = s_total // l
    assert s_total % l == 0 and g_total % gt == 0 and l % SUBLANES == 0
    assert gt == 1 or n_s == 1
    row_spec = lambda c: pl.BlockSpec((rows, c), lambda g, s: (g * n_s + s, 0))
    f32 = jnp.float32
    if transposed:
        assert gt == 1
        qkv_spec = pl.BlockSpec((1, D_ATTN, l), lambda g, s: (g, 0, s))
        qkv_shape = jax.ShapeDtypeStruct((g_total, D_ATTN, s_total), f32)
    else:
        qkv_spec = row_spec(D_ATTN)
        qkv_shape = jax.ShapeDtypeStruct((n, D_ATTN), f32)
    return pl.pallas_call(
        functools.partial(_inproj_conv_kernel, gt=gt, l=l, width=width, transposed=transposed),
        grid=(g_total // gt, n_s),
        in_specs=[
            row_spec(d),
            pl.BlockSpec((gt, width - 1, dc), lambda g, s: (g, 0, 0)),
            _const_spec((1, d)), _const_spec(wqkv.shape), _const_spec(bqkv.shape),
            _const_spec(wcv.shape), _const_spec((1, 2 * dc)),
            _const_spec(wdw.shape), _const_spec((1, dc)), _const_spec((1, dc)), _const_spec((1, dc)),
            _const_spec(wpw.shape), _const_spec((1, dc)),
        ],
        out_specs=[qkv_spec, qkv_spec, qkv_spec, row_spec(dc),
                   pl.BlockSpec((gt, width - 1, dc), lambda g, s: (g, 0, 0))],
        out_shape=[qkv_shape, qkv_shape, qkv_shape, jax.ShapeDtypeStruct((n, dc), jnp.bfloat16),
                   jax.ShapeDtypeStruct((g_total, width - 1, dc), f32)],
        scratch_shapes=[pltpu.VMEM((gt, pad + l, dc), f32), pltpu.VMEM((rows, dc), f32),
                        pltpu.VMEM((SUBLANES - 1, min(l, CONV_CHUNK) + pad - SUBLANES, dc), f32)],
        compiler_params=pltpu.CompilerParams(
            dimension_semantics=("arbitrary", "arbitrary"), vmem_limit_bytes=VMEM_LIMIT),
        name="inproj_conv",
    )(x.reshape(n, d), hist, gpre.reshape(1, d), wqkv, bqkv, wcv, bcv.reshape(1, 2 * dc), wdw,
      bdw.reshape(1, dc), gln.reshape(1, dc), bln.reshape(1, dc), wpw, bpw.reshape(1, dc))


def _moba_prompt_kernel(qt_ref, kt_ref, vt_ref, o_ref, kaug_ref, vaug_ref, s_ref, p_ref, *, nb):
    blk = MOBA_BLOCK
    s_len = kt_ref.shape[2]
    qscale = HEAD_DIM ** -0.5 * LOG2E
    f32 = jnp.float32
    n_buf = s_ref.shape[0]

    kt = kt_ref[0]
    vt = vt_ref[0]
    key_blk = lax.broadcasted_iota(jnp.int32, (nb, s_len), 1) // blk
    ind = jnp.where(key_blk == lax.broadcasted_iota(jnp.int32, (nb, s_len), 0), 1.0 / blk, 0.0)
    km = lax.dot_general(ind, kt, _NT, precision=_HI, preferred_element_type=f32)
    km_lane = lax.broadcasted_iota(jnp.int32, (nb, LANES), 1)

    k_rows = kt.T
    row_blk = lax.broadcasted_iota(jnp.int32, (s_len, LANES), 0) // blk
    lane_s = lax.broadcasted_iota(jnp.int32, (s_len, LANES), 1)
    sub_s = lax.broadcasted_iota(jnp.int32, (LANES, s_len), 0)
    for hh in range(2):
        own_lo = hh * HEAD_DIM
        base = HEAD_DIM - own_lo
        own_s = (lane_s >= own_lo) & (lane_s < own_lo + HEAD_DIM)
        onehot = (lane_s - base == row_blk).astype(f32)
        kaug_ref[hh] = jnp.where(own_s, k_rows, onehot).astype(jnp.bfloat16)
        own_r = (sub_s >= own_lo) & (sub_s < own_lo + HEAD_DIM)
        vaug_ref[hh] = jnp.where(own_r, vt, jnp.where(sub_s == base, 1.0, 0.0)).astype(jnp.bfloat16)

    sub = lax.broadcasted_iota(jnp.int32, (nb, blk), 0)
    rr = lax.broadcasted_iota(jnp.int32, (blk, blk), 0)
    cc = lax.broadcasted_iota(jnp.int32, (blk, blk), 1)
    causal = rr <= cc
    zeros_fill = jnp.zeros((HEAD_DIM - nb, blk), f32)
    items = [(i, hh) for i in range(nb) for hh in range(2)]
    o_heads = {}

    def logits(n):
        i, hh = items[n]
        w = (i + 1) * blk
        slot = n % n_buf
        own_lo = hh * HEAD_DIM
        qt_i = qt_ref[0, :, i * blk:(i + 1) * blk]
        km_h = jnp.where((km_lane >= own_lo) & (km_lane < own_lo + HEAD_DIM), km, 0.0)
        gate = jnp.dot(km_h, qt_i, precision=_HI, preferred_element_type=f32)
        cnt = jnp.zeros((nb, blk), jnp.int32)
        for m in range(i):
            row = gate[m:m + 1, :]
            beats = (row > gate) | ((row == gate) & (m < sub))
            cnt = cnt + beats.astype(jnp.int32)
        keep = ((sub < i) & (cnt < MOBA_TOPK)) | (sub == i)
        bias = jnp.where(keep, 0.0, NEG)
        q_own = qt_i[own_lo:own_lo + HEAD_DIM, :] * qscale
        parts = [q_own, bias, zeros_fill] if hh == 0 else [bias, zeros_fill, q_own]
        q_aug = jnp.concatenate(parts, axis=0).astype(jnp.bfloat16)
        if i > 0:
            s_ref[slot, 0:w - blk, :] = jnp.dot(kaug_ref[hh, 0:w - blk, :], q_aug, preferred_element_type=f32)
        s_own = jnp.dot(kaug_ref[hh, w - blk:w, :], q_aug, preferred_element_type=f32)
        s_ref[slot, w - blk:w, :] = jnp.where(causal, s_own, NEG)

    def softmax(n):
        i, _ = items[n]
        w = (i + 1) * blk
        slot = n % n_buf
        sv = s_ref[slot, 0:w, :]
        mx = jnp.max(sv, axis=0, keepdims=True)
        p_ref[slot, 0:w, :] = jnp.exp2(sv - mx).astype(jnp.bfloat16)

    def values(n):
        i, hh = items[n]
        w = (i + 1) * blk
        slot = n % n_buf
        own_lo = hh * HEAD_DIM
        base = HEAD_DIM - own_lo
        o_t = jnp.dot(vaug_ref[hh, :, 0:w], p_ref[slot, 0:w, :], preferred_element_type=f32)
        o_heads[hh] = o_t[own_lo:own_lo + HEAD_DIM, :] * (1.0 / o_t[base:base + 1, :])
        if hh == 1:
            o_pair = jnp.concatenate([o_heads[0], o_heads[1]], axis=0)
            o_ref[0, i * blk:(i + 1) * blk, :] = o_pair.T.astype(o_ref.dtype)

    for n in range(len(items) + 2):
        if n < len(items):
            logits(n)
        if 0 <= n - 1 < len(items):
            softmax(n - 1)
        if 0 <= n - 2 < len(items):
            values(n - 2)


def _moba_prompt(qt, kt, vt):
    b, da, s_len = qt.shape
    assert s_len % MOBA_BLOCK == 0 and da % LANES == 0 and LANES == 2 * HEAD_DIM
    nb = s_len // MOBA_BLOCK
    assert nb <= HEAD_DIM
    n_buf = 3
    spec = pl.BlockSpec((1, LANES, s_len), lambda bi, hp: (bi, hp, 0))
    return pl.pallas_call(
        functools.partial(_moba_prompt_kernel, nb=nb),
        grid=(b, da // LANES),
        in_specs=[spec, spec, spec],
        out_specs=pl.BlockSpec((1, s_len, LANES), lambda bi, hp: (bi, 0, hp)),
        out_shape=jax.ShapeDtypeStruct((b, s_len, da), jnp.bfloat16),
        scratch_shapes=[pltpu.VMEM((2, s_len, LANES), jnp.bfloat16),
                        pltpu.VMEM((2, LANES, s_len), jnp.bfloat16),
                        pltpu.VMEM((n_buf, s_len, MOBA_BLOCK), jnp.float32),
                        pltpu.VMEM((n_buf, s_len, MOBA_BLOCK), jnp.bfloat16)],
        compiler_params=pltpu.CompilerParams(
            dimension_semantics=("arbitrary", "arbitrary"), vmem_limit_bytes=VMEM_LIMIT),
        name="moba_prompt",
    )(qt, kt, vt)


def _fold_heads(o, row_head):
    r, da = o.shape
    lane_head = lax.broadcasted_iota(jnp.int32, (r, da), 1) // HEAD_DIM
    om = jnp.where(lane_head == row_head, o, 0.0)
    out = om[:, 0:LANES]
    for p in range(1, da // LANES):
        out = out + om[:, p * LANES:(p + 1) * LANES]
    return out


def _moba_sample_kernel(pt_ref, q_ref, kn_ref, vn_ref, *rest, t, npb, ppb, pps):
    kp_refs = rest[0:pps]
    vp_refs = rest[pps:2 * pps]
    o_ref = rest[2 * pps]
    g_ref, m_ref, l_ref, op_ref = rest[2 * pps + 1:]
    del pt_ref
    step = pl.program_id(1)
    r = N_HEADS * t
    da = q_ref.shape[-1]
    bps = pps // ppb
    scale = HEAD_DIM ** -0.5 * LOG2E
    f32 = jnp.float32

    q = q_ref[...]
    q_rep = jnp.concatenate([q] * N_HEADS, axis=0)
    row_head = lax.broadcasted_iota(jnp.int32, (r, da), 0) // t
    lane_head = lax.broadcasted_iota(jnp.int32, (r, da), 1) // HEAD_DIM
    q_exp = jnp.where(row_head == lane_head, q_rep, 0.0)
    q_s = q_exp * scale
    q_log = q_s.astype(jnp.bfloat16)
    q_lo = (q_s - q_log.astype(f32)).astype(jnp.bfloat16)
    q_two = jnp.concatenate([q_log, q_lo], axis=0)
    lane = lax.broadcasted_iota(jnp.int32, (r, LANES), 1)

    @pl.when(step == 0)
    def _():
        g_ref[...] = jnp.zeros(g_ref.shape, g_ref.dtype)
        m_ref[...] = jnp.full(m_ref.shape, NEG, m_ref.dtype)
        l_ref[...] = jnp.zeros(l_ref.shape, l_ref.dtype)

    g_new, m_new, l_new = g_ref[...], m_ref[...], l_ref[...]
    for bb in range(bps):
        n = step * bps + bb
        kb = jnp.concatenate([kp_refs[bb * ppb + p][0].astype(jnp.bfloat16) for p in range(ppb)], axis=1)
        vb = jnp.concatenate([vp_refs[bb * ppb + p][0].astype(jnp.bfloat16) for p in range(ppb)], axis=1)
        s_two = jnp.dot(q_two, kb, preferred_element_type=f32)
        sb = s_two[0:r]
        gb = jnp.sum(sb + s_two[r:2 * r], axis=-1, keepdims=True)
        mb = jnp.max(sb, axis=-1, keepdims=True)
        pb = jnp.exp2(sb - mb)
        lb = jnp.sum(pb, axis=-1, keepdims=True)
        ob = lax.dot_general(pb.astype(jnp.bfloat16), vb, _NT, preferred_element_type=f32)
        op_ref[n] = _fold_heads(ob, row_head)
        g_new = jnp.where(lane == n, gb, g_new)
        m_new = jnp.where(lane == n, mb, m_new)
        l_new = jnp.where(lane == n, lb, l_new)
    g_ref[...] = g_new
    m_ref[...] = m_new
    l_ref[...] = l_new

    @pl.when(step == pl.num_programs(1) - 1)
    def _():
        sc = jnp.where(lane < npb, g_ref[...], -jnp.inf)
        sel = jnp.zeros((r, LANES), jnp.bool_)
        for _ in range(min(MOBA_TOPK, npb)):
            best = jnp.max(sc, axis=-1, keepdims=True)
            first = jnp.min(jnp.where(sc == best, lane, LANES), axis=-1, keepdims=True)
            hit = lane == first
            sel = sel | hit
            sc = jnp.where(hit, -jnp.inf, sc)

        kn = kn_ref[...]
        vn = vn_ref[...]
        so = lax.dot_general(q_log, kn.astype(jnp.bfloat16), _NT, preferred_element_type=f32)
        tq = lax.broadcasted_iota(jnp.int32, (r, t), 0) % t
        tk = lax.broadcasted_iota(jnp.int32, (r, t), 1)
        so = jnp.where(tk <= tq, so, NEG)
        m_own = jnp.max(so, axis=-1, keepdims=True)
        p_own = jnp.exp2(so - m_own)
        l_own = jnp.sum(p_own, axis=-1, keepdims=True)
        o_own = _fold_heads(jnp.dot(p_own.astype(jnp.bfloat16), vn.astype(jnp.bfloat16),
                                    preferred_element_type=f32), row_head)

        m_sel = jnp.where(sel, m_ref[...], NEG)
        m_tot = jnp.maximum(m_own, jnp.max(m_sel, axis=-1, keepdims=True))
        w = jnp.where(sel, jnp.exp2(m_sel - m_tot), 0.0)
        w_own = jnp.exp2(m_own - m_tot)
        l_tot = w_own * l_own + jnp.sum(w * l_ref[...], axis=-1, keepdims=True)

        def add_block(nn, acc):
            wn = jnp.sum(jnp.where(lane == nn, w, 0.0), axis=-1, keepdims=True)
            return acc + wn * op_ref[nn]

        o_tot = lax.fori_loop(0, npb, add_block, w_own * o_own) / l_tot
        tiles = []
        for p in range(da // LANES):
            lo = o_tot[(2 * p) * t:(2 * p + 1) * t, :]
            hi = o_tot[(2 * p + 1) * t:(2 * p + 2) * t, :]
            tiles.append(jnp.where(lane[0:t] < HEAD_DIM, lo, hi))
        o_ref[...] = jnp.concatenate(tiles, axis=-1).astype(o_ref.dtype)


def _moba_sample(q, k_new, v_new, pool_k, pool_v, page_table, *, t):
    n, da = q.shape
    db, n_pages = page_table.shape
    n_phys, page = pool_k.shape[0], pool_k.shape[1]
    ppb = MOBA_BLOCK // page
    npb = (n_pages * page) // MOBA_BLOCK
    assert MOBA_BLOCK % page == 0 and n_pages == npb * ppb and npb >= 1 and npb <= LANES
    pps = PAGES_PER_STEP
    assert pps % ppb == 0 and n_pages % pps == 0 and t % SUBLANES == 0 and page % LANES == 0
    pk = jnp.transpose(pool_k, (0, 2, 3, 1)).reshape(n_phys, da, page)
    pv = jnp.transpose(pool_v, (0, 2, 3, 1)).reshape(n_phys, da, page)
    r = N_HEADS * t

    def page_spec(j):
        return pl.BlockSpec((1, da, page), lambda d, s, pt: (pt[d, s * pps + j], 0, 0))

    row_spec = pl.BlockSpec((t, da), lambda d, s, pt: (d, 0))
    grid_spec = pltpu.PrefetchScalarGridSpec(
        num_scalar_prefetch=1,
        grid=(db, n_pages // pps),
        in_specs=[row_spec, row_spec, row_spec] + [page_spec(j) for j in range(pps)] * 2,
        out_specs=row_spec,
        scratch_shapes=[pltpu.VMEM((r, LANES), jnp.float32),
                        pltpu.VMEM((r, LANES), jnp.float32),
                        pltpu.VMEM((r, LANES), jnp.float32),
                        pltpu.VMEM((npb, r, LANES), jnp.float32)],
    )
    return pl.pallas_call(
        functools.partial(_moba_sample_kernel, t=t, npb=npb, ppb=ppb, pps=pps),
        grid_spec=grid_spec,
        out_shape=jax.ShapeDtypeStruct((n, da), jnp.bfloat16),
        compiler_params=pltpu.CompilerParams(
            dimension_semantics=("arbitrary", "arbitrary"), vmem_limit_bytes=VMEM_LIMIT),
        name="moba_sample",
    )(page_table, q, k_new, v_new, *([pk] * pps), *([pv] * pps))


def _out_ffn_kernel(x_ref, a_ref, c_ref, hist_ref, wo_ref, gpm_ref, gpf_ref, wup_ref, wdw_ref, bdw_ref,
                    wdn_ref, gpo_ref, y_ref, st_ref, carry_ref, ug_ref, uv_ref, f_ref, *, gt, l, width):
    s = pl.program_id(1)
    d_ff = wdn_ref.shape[0]
    da = a_ref.shape[-1]
    pad = carry_ref.shape[1]
    h0 = pad - (width - 1)

    @pl.when(s == 0)
    def _():
        carry_ref[:, h0:pad, :] = hist_ref[...]

    mix = (jnp.dot(a_ref[...], wo_ref[0:da, :], preferred_element_type=jnp.float32)
           + jnp.dot(c_ref[...], wo_ref[da:, :], preferred_element_type=jnp.float32))
    x1 = x_ref[...] + _rms(mix, gpm_ref[...])
    h = _rms(x1, gpf_ref[...]).astype(jnp.bfloat16)

    def up_conv(buf, lo):
        hi = lo + FF_CHUNK
        buf[:, 0:pad, :] = carry_ref[:, :, lo:hi]
        u = jnp.dot(h, wup_ref[:, lo:hi], preferred_element_type=jnp.float32)
        buf[:, pad:pad + l, :] = u.reshape(gt, l, FF_CHUNK)
        carry_ref[:, :, lo:hi] = buf[:, l:l + pad, :]
        acc = buf[:, h0:h0 + l, :] * wdw_ref[0:1, lo:hi]
        for j in range(1, width):
            acc = acc + buf[:, h0 + j:h0 + j + l, :] * wdw_ref[j:j + 1, lo:hi]
        return (acc + bdw_ref[:, lo:hi]).reshape(gt * l, FF_CHUNK)

    for cc in range(d_ff // FF_CHUNK):
        gate = up_conv(ug_ref, cc * FF_CHUNK)
        val = up_conv(uv_ref, d_ff + cc * FF_CHUNK)
        f_ref[:, cc * FF_CHUNK:(cc + 1) * FF_CHUNK] = (_silu(gate) * val).astype(jnp.bfloat16)

    ff = jnp.dot(f_ref[...], wdn_ref[...], preferred_element_type=jnp.float32)
    y_ref[...] = x1 + _rms(ff, gpo_ref[...])

    @pl.when(s == pl.num_programs(1) - 1)
    def _():
        st_ref[...] = carry_ref[:, h0:pad, :]


def _out_ffn(x, attn, c, hist, wo, gpm, gpf, wup, wdw, bdw, wdn, gpo, *, gt, l):
    g_total, s_total, d = x.shape
    n = g_total * s_total
    d_ff = wdn.shape[0]
    width = wdw.shape[0]
    pad = SUBLANES
    assert width - 1 <= pad <= l and d_ff % FF_CHUNK == 0
    rows = gt * l
    n_s = s_total // l
    assert s_total % l == 0 and g_total % gt == 0 and l % SUBLANES == 0
    assert gt == 1 or n_s == 1
    row_spec = lambda cdim: pl.BlockSpec((rows, cdim), lambda g, s: (g * n_s + s, 0))
    st_spec = pl.BlockSpec((gt, width - 1, 2 * d_ff), lambda g, s: (g, 0, 0))
    f32 = jnp.float32
    return pl.pallas_call(
        functools.partial(_out_ffn_kernel, gt=gt, l=l, width=width),
        grid=(g_total // gt, n_s),
        in_specs=[
            row_spec(d), row_spec(attn.shape[1]), row_spec(c.shape[1]), st_spec,
            _const_spec(wo.shape), _const_spec((1, d)), _const_spec((1, d)),
            _const_spec(wup.shape), _const_spec(wdw.shape), _const_spec((1, 2 * d_ff)),
            _const_spec(wdn.shape), _const_spec((1, d)),
        ],
        out_specs=[row_spec(d), st_spec],
        out_shape=[jax.ShapeDtypeStruct((n, d), f32),
                   jax.ShapeDtypeStruct((g_total, width - 1, 2 * d_ff), f32)],
        scratch_shapes=[pltpu.VMEM((gt, pad, 2 * d_ff), f32),
                        pltpu.VMEM((gt, pad + l, FF_CHUNK), f32),
                        pltpu.VMEM((gt, pad + l, FF_CHUNK), f32),
                        pltpu.VMEM((rows, d_ff), jnp.bfloat16)],
        compiler_params=pltpu.CompilerParams(
            dimension_semantics=("arbitrary", "arbitrary"), vmem_limit_bytes=VMEM_LIMIT),
        name="out_ffn",
    )(x.reshape(n, d), attn, c, hist, wo, gpm.reshape(1, d), gpf.reshape(1, d), wup, wdw,
      bdw.reshape(1, -1), wdn, gpo.reshape(1, d))


def kernel(x_prompt, x_sample, cache_k, cache_v, page_table, state_cv, state_ffn, g_pre_mix, w_in, b_in,
           w_cv_dw, b_cv_dw, g_cv_ln, b_cv_ln, w_cv_pw, b_cv_pw, w_out, g_post_mix, g_pre_ffn, w_up,
           w_ffn_dw, b_ffn_dw, w_down, g_post_ffn):
    depth = w_in.shape[0]
    bf16 = jnp.bfloat16
    b, s_len, d = x_prompt.shape
    db, t, _ = x_sample.shape
    n_qkv = 3 * D_ATTN
    kv_s = (db, t, N_HEADS, HEAD_DIM)
    yp, ys = x_prompt, x_sample
    outs = [[] for _ in range(8)]
    for lyr in range(depth):
        w_qkv = w_in[lyr][:, :n_qkv].astype(bf16)
        b_qkv = b_in[lyr][:n_qkv]
        cv_prm = (w_in[lyr][:, n_qkv:].astype(bf16), b_in[lyr][n_qkv:], w_cv_dw[lyr], b_cv_dw[lyr],
                  g_cv_ln[lyr], b_cv_ln[lyr], w_cv_pw[lyr].astype(bf16), b_cv_pw[lyr])
        ffn_prm = (w_out[lyr].astype(bf16), g_post_mix[lyr], g_pre_ffn[lyr], w_up[lyr].astype(bf16),
                   w_ffn_dw[lyr], b_ffn_dw[lyr], w_down[lyr].astype(bf16), g_post_ffn[lyr])

        cv0 = jnp.zeros((b,) + state_cv.shape[2:], x_prompt.dtype)
        ff0 = jnp.zeros((b,) + state_ffn.shape[2:], x_prompt.dtype)
        lp = min(ROW_TILE, s_len)
        qt, kt, vt, c, cvp = _inproj_conv(yp, cv0, g_pre_mix[lyr], w_qkv.T, b_qkv.reshape(n_qkv, 1), *cv_prm,
                                          gt=1, l=lp, transposed=True)
        attn = _moba_prompt(qt, kt, vt).reshape(b * s_len, D_ATTN)
        yp, ffp = _out_ffn(yp, attn, c, ff0, *ffn_prm, gt=1, l=lp)
        yp = yp.reshape(b, s_len, d)
        kp = jnp.transpose(kt.reshape(b, N_HEADS, HEAD_DIM, s_len), (0, 3, 1, 2))
        vp = jnp.transpose(vt.reshape(b, N_HEADS, HEAD_DIM, s_len), (0, 3, 1, 2))

        q, k, v, c, cvs = _inproj_conv(ys, state_cv[lyr], g_pre_mix[lyr], w_qkv, b_qkv.reshape(1, n_qkv), *cv_prm,
                                       gt=db, l=t, transposed=False)
        attn = _moba_sample(q, k, v, cache_k[lyr], cache_v[lyr], page_table, t=t)
        ys, ffs = _out_ffn(ys, attn, c, state_ffn[lyr], *ffn_prm, gt=db, l=t)
        ys = ys.reshape(db, t, d)
        for lst, val in zip(outs, (kp, vp, cvp, ffp, k.reshape(kv_s), v.reshape(kv_s), cvs, ffs)):
            lst.append(val)
    return (yp, ys) + tuple(jnp.stack(lst) for lst in outs)
```

```python
import functools

import jax
import jax.numpy as jnp
from jax import lax
from jax.experimental import pallas as pl
from jax.experimental.pallas import tpu as pltpu

N_HEADS = 8
HEAD_DIM = 64
D_ATTN = N_HEADS * HEAD_DIM
MOBA_BLOCK = 256
MOBA_TOPK = 3
RMS_EPS = 1e-6
LN_EPS = 1e-5
NEG = -1e30
LOG2E = 1.4426950408889634

LANES = 128
SUBLANES = 8
MXU_DIM = 256
VMEM_LIMIT = 56 * 1024 * 1024

ROW_TILE = 512
CONV_CHUNK = 32
FF_CHUNK = 256
PAGES_PER_STEP = 16

_NT = (((1,), (1,)), ((), ()))
_HI = lax.Precision.HIGHEST


def _const_spec(shape):
    nd = len(shape)
    return pl.BlockSpec(shape, lambda *_: (0,) * nd, pipeline_mode=pl.Buffered(1))


def _rms(x, g):
    return x * lax.rsqrt(jnp.mean(x * x, axis=-1, keepdims=True) + RMS_EPS) * g


def _silu(x):
    return x / (1.0 + jnp.exp(-x))


def _inproj_conv_kernel(x_ref, hist_ref, gpre_ref, wqkv_ref, bqkv_ref, wcv_ref, bcv_ref, wdw_ref, bdw_ref,
                        gln_ref, bln_ref, wpw_ref, bpw_ref,
                        q_ref, k_ref, v_ref, c_ref, cvs_ref,
                        ext_ref, act_ref, *, gt, l, width, transposed):
    s = pl.program_id(1)
    dc = act_ref.shape[-1]
    pad = ext_ref.shape[1] - l
    h0 = pad - (width - 1)

    @pl.when(s == 0)
    def _():
        ext_ref[:, h0:pad, :] = hist_ref[...]

    h = _rms(x_ref[...], gpre_ref[...]).astype(jnp.bfloat16)

    cva = jnp.dot(h, wcv_ref[:, 0:dc], preferred_element_type=jnp.float32) + bcv_ref[:, 0:dc]
    cvg = jnp.dot(h, wcv_ref[:, dc:2 * dc], preferred_element_type=jnp.float32) + bcv_ref[:, dc:2 * dc]
    glu = cva / (1.0 + jnp.exp(-cvg))
    ext_ref[:, pad:pad + l, :] = glu.reshape(gt, l, dc)

    def qkv_proj(idx, ref):
        lo, hi = idx * D_ATTN, (idx + 1) * D_ATTN
        if transposed:
            ref[0] = lax.dot_general(wqkv_ref[lo:hi, :], h, _NT,
                                     preferred_element_type=jnp.float32) + bqkv_ref[lo:hi, :]
        else:
            ref[...] = jnp.dot(h, wqkv_ref[:, lo:hi], preferred_element_type=jnp.float32) + bqkv_ref[:, lo:hi]

    rc = min(l, CONV_CHUNK)
    n_chunks = l // rc
    chunks = [(g, cidx * rc) for g in range(gt) for cidx in range(n_chunks)]
    pending = [(len(chunks) * (idx + 1) // 4, idx, ref) for idx, ref in enumerate((q_ref, k_ref, v_ref))]
    for it in range(len(chunks) + 1):
        while pending and (it == len(chunks) or pending[0][0] <= it):
            _, idx, ref = pending.pop(0)
            qkv_proj(idx, ref)
        if it == len(chunks):
            break
        g, r0 = chunks[it]
        for lt in range(dc // LANES):
            ls = slice(lt * LANES, (lt + 1) * LANES)
            win = ext_ref[g, r0:r0 + rc + pad, ls]
            shifted = {r: pltpu.roll(win, rc + pad - r, 0) for r in range(1, SUBLANES)}
            acc = None
            for j in range(width):
                a, r = divmod(h0 + j, SUBLANES)
                lo = a * SUBLANES
                tap = ext_ref[g, r0 + lo:r0 + lo + rc, ls] if r == 0 else shifted[r][lo:lo + rc]
                term = tap * wdw_ref[j:j + 1, ls]
                acc = term if acc is None else acc + term
            act_ref[g * l + r0:g * l + r0 + rc, ls] = acc + bdw_ref[:, ls]
        cv = act_ref[g * l + r0:g * l + r0 + rc, :]
        mu = jnp.mean(cv, axis=-1, keepdims=True)
        cen = cv - mu
        var = jnp.mean(cen * cen, axis=-1, keepdims=True)
        y = cen * lax.rsqrt(var + LN_EPS) * gln_ref[...] + bln_ref[...]
        act_ref[g * l + r0:g * l + r0 + rc, :] = _silu(y)

    c = jnp.dot(act_ref[...].astype(jnp.bfloat16), wpw_ref[...],
                preferred_element_type=jnp.float32) + bpw_ref[...]
    c_ref[...] = c.astype(c_ref.dtype)

    @pl.when(s == pl.num_programs(1) - 1)
    def _():
        cvs_ref[...] = ext_ref[:, l + h0:l + pad, :]

    ext_ref[:, 0:pad, :] = ext_ref[:, l:l + pad, :]


def _inproj_conv(x, hist, gpre, wqkv, bqkv, wcv, bcv, wdw, bdw, gln, bln, wpw, bpw, *, gt, l, transposed):
    g_total, s_total, d = x.shape
    n = g_total * s_total
    dc = wpw.shape[0]
    width = wdw.shape[0]
    pad = -(-(width - 1) // SUBLANES) * SUBLANES
    rows = gt * l
    n_s = s_total // l
    assert s_total % l == 0 and g_total % gt == 0 and l % SUBLANES == 0
    assert gt == 1 or n_s == 1
    row_spec = lambda c: pl.BlockSpec((rows, c), lambda g, s: (g * n_s + s, 0))
    f32 = jnp.float32
    if transposed:
        assert gt == 1
        qkv_spec = pl.BlockSpec((1, D_ATTN, l), lambda g, s: (g, 0, s))
        qkv_shape = jax.ShapeDtypeStruct((g_total, D_ATTN, s_total), f32)
    else:
        qkv_spec = row_spec(D_ATTN)
        qkv_shape = jax.ShapeDtypeStruct((n, D_ATTN), f32)
    return pl.pallas_call(
        functools.partial(_inproj_conv_kernel, gt=gt, l=l, width=width, transposed=transposed),
        grid=(g_total // gt, n_s),
        in_specs=[
            row_spec(d),
            pl.BlockSpec((gt, width - 1, dc), lambda g, s: (g, 0, 0)),
            _const_spec((1, d)), _const_spec(wqkv.shape), _const_spec(bqkv.shape),
            _const_spec(wcv.shape), _const_spec((1, 2 * dc)),
            _const_spec(wdw.shape), _const_spec((1, dc)), _const_spec((1, dc)), _const_spec((1, dc)),
            _const_spec(wpw.shape), _const_spec((1, dc)),
        ],
        out_specs=[qkv_spec, qkv_spec, qkv_spec, row_spec(dc),
                   pl.BlockSpec((gt, width - 1, dc), lambda g, s: (g, 0, 0))],
        out_shape=[qkv_shape, qkv_shape, qkv_shape, jax.ShapeDtypeStruct((n, dc), jnp.bfloat16),
                   jax.ShapeDtypeStruct((g_total, width - 1, dc), f32)],
        scratch_shapes=[pltpu.VMEM((gt, pad + l, dc), f32), pltpu.VMEM((rows, dc), f32)],
        compiler_params=pltpu.CompilerParams(
            dimension_semantics=("arbitrary", "arbitrary"), vmem_limit_bytes=VMEM_LIMIT),
        name="inproj_conv",
    )(x.reshape(n, d), hist, gpre.reshape(1, d), wqkv, bqkv, wcv, bcv.reshape(1, 2 * dc), wdw,
      bdw.reshape(1, dc), gln.reshape(1, dc), bln.reshape(1, dc), wpw, bpw.reshape(1, dc))


def _moba_prompt_kernel(qt_ref, kt_ref, vt_ref, o_ref, kaug_ref, vaug_ref, s_ref, p_ref, *, nb):
    blk = MOBA_BLOCK
    s_len = kt_ref.shape[2]
    qscale = HEAD_DIM ** -0.5 * LOG2E
    f32 = jnp.float32
    n_buf = s_ref.shape[0]

    kt = kt_ref[0]
    vt = vt_ref[0]
    key_blk = lax.broadcasted_iota(jnp.int32, (nb, s_len), 1) // blk
    ind = jnp.where(key_blk == lax.broadcasted_iota(jnp.int32, (nb, s_len), 0), 1.0 / blk, 0.0)
    km = lax.dot_general(ind, kt, _NT, precision=_HI, preferred_element_type=f32)
    km_lane = lax.broadcasted_iota(jnp.int32, (nb, LANES), 1)

    k_rows = kt.T
    row_blk = lax.broadcasted_iota(jnp.int32, (s_len, LANES), 0) // blk
    lane_s = lax.broadcasted_iota(jnp.int32, (s_len, LANES), 1)
    sub_s = lax.broadcasted_iota(jnp.int32, (LANES, s_len), 0)
    for hh in range(2):
        own_lo = hh * HEAD_DIM
        base = HEAD_DIM - own_lo
        own_s = (lane_s >= own_lo) & (lane_s < own_lo + HEAD_DIM)
        onehot = (lane_s - base == row_blk).astype(f32)
        kaug_ref[hh] = jnp.where(own_s, k_rows, onehot).astype(jnp.bfloat16)
        own_r = (sub_s >= own_lo) & (sub_s < own_lo + HEAD_DIM)
        vaug_ref[hh] = jnp.where(own_r, vt, jnp.where(sub_s == base, 1.0, 0.0)).astype(jnp.bfloat16)

    sub = lax.broadcasted_iota(jnp.int32, (nb, blk), 0)
    rr = lax.broadcasted_iota(jnp.int32, (blk, blk), 0)
    cc = lax.broadcasted_iota(jnp.int32, (blk, blk), 1)
    causal = rr <= cc
    zeros_fill = jnp.zeros((HEAD_DIM - nb, blk), f32)
    items = [(i, hh) for i in range(nb) for hh in range(2)]
    o_heads = {}

    def logits(n):
        i, hh = items[n]
        w = (i + 1) * blk
        slot = n % n_buf
        own_lo = hh * HEAD_DIM
        qt_i = qt_ref[0, :, i * blk:(i + 1) * blk]
        km_h = jnp.where((km_lane >= own_lo) & (km_lane < own_lo + HEAD_DIM), km, 0.0)
        gate = jnp.dot(km_h, qt_i, precision=_HI, preferred_element_type=f32)
        cnt = jnp.zeros((nb, blk), jnp.int32)
        for m in range(i):
            row = gate[m:m + 1, :]
            beats = (row > gate) | ((row == gate) & (m < sub))
            cnt = cnt + beats.astype(jnp.int32)
        keep = ((sub < i) & (cnt < MOBA_TOPK)) | (sub == i)
        bias = jnp.where(keep, 0.0, NEG)
        q_own = qt_i[own_lo:own_lo + HEAD_DIM, :] * qscale
        parts = [q_own, bias, zeros_fill] if hh == 0 else [bias, zeros_fill, q_own]
        q_aug = jnp.concatenate(parts, axis=0).astype(jnp.bfloat16)
        if i > 0:
            s_ref[slot, 0:w - blk, :] = jnp.dot(kaug_ref[hh, 0:w - blk, :], q_aug, preferred_element_type=f32)
        s_own = jnp.dot(kaug_ref[hh, w - blk:w, :], q_aug, preferred_element_type=f32)
        s_ref[slot, w - blk:w, :] = jnp.where(causal, s_own, NEG)

    def softmax(n):
        i, _ = items[n]
        w = (i + 1) * blk
        slot = n % n_buf
        sv = s_ref[slot, 0:w, :]
        mx = jnp.max(sv, axis=0, keepdims=True)
        p_ref[slot, 0:w, :] = jnp.exp2(sv - mx).astype(jnp.bfloat16)

    def values(n):
        i, hh = items[n]
        w = (i + 1) * blk
        slot = n % n_buf
        own_lo = hh * HEAD_DIM
        base = HEAD_DIM - own_lo
        o_t = jnp.dot(vaug_ref[hh, :, 0:w], p_ref[slot, 0:w, :], preferred_element_type=f32)
        o_heads[hh] = o_t[own_lo:own_lo + HEAD_DIM, :] * (1.0 / o_t[base:base + 1, :])
        if hh == 1:
            o_pair = jnp.concatenate([o_heads[0], o_heads[1]], axis=0)
            o_ref[0, i * blk:(i + 1) * blk, :] = o_pair.T.astype(o_ref.dtype)

    lag = (n_buf - 1) // 2
    for n in range(len(items) + 2 * lag):
        if n < len(items):
            logits(n)
        if 0 <= n - lag < len(items):
            softmax(n - lag)
        if 0 <= n - 2 * lag < len(items):
            values(n - 2 * lag)


def _moba_prompt(qt, kt, vt):
    b, da, s_len = qt.shape
    assert s_len % MOBA_BLOCK == 0 and da % LANES == 0 and LANES == 2 * HEAD_DIM
    nb = s_len // MOBA_BLOCK
    assert nb <= HEAD_DIM
    n_buf = 5
    spec = pl.BlockSpec((1, LANES, s_len), lambda bi, hp: (bi, hp, 0))
    return pl.pallas_call(
        functools.partial(_moba_prompt_kernel, nb=nb),
        grid=(b, da // LANES),
        in_specs=[spec, spec, spec],
        out_specs=pl.BlockSpec((1, s_len, LANES), lambda bi, hp: (bi, 0, hp)),
        out_shape=jax.ShapeDtypeStruct((b, s_len, da), jnp.bfloat16),
        scratch_shapes=[pltpu.VMEM((2, s_len, LANES), jnp.bfloat16),
                        pltpu.VMEM((2, LANES, s_len), jnp.bfloat16),
                        pltpu.VMEM((n_buf, s_len, MOBA_BLOCK), jnp.float32),
                        pltpu.VMEM((n_buf, s_len, MOBA_BLOCK), jnp.bfloat16)],
        compiler_params=pltpu.CompilerParams(
            dimension_semantics=("arbitrary", "arbitrary"), vmem_limit_bytes=VMEM_LIMIT),
        name="moba_prompt",
    )(qt, kt, vt)


def _fold_heads(o, row_head):
    r, da = o.shape
    lane_head = lax.broadcasted_iota(jnp.int32, (r, da), 1) // HEAD_DIM
    om = jnp.where(lane_head == row_head, o, 0.0)
    out = om[:, 0:LANES]
    for p in range(1, da // LANES):
        out = out + om[:, p * LANES:(p + 1) * LANES]
    return out


def _moba_sample_kernel(pt_ref, q_ref, kn_ref, vn_ref, *rest, t, npb, ppb, pps):
    kp_refs = rest[0:pps]
    vp_refs = rest[pps:2 * pps]
    o_ref = rest[2 * pps]
    g_ref, m_ref, l_ref, op_ref = rest[2 * pps + 1:]
    del pt_ref
    step = pl.program_id(1)
    r = N_HEADS * t
    da = q_ref.shape[-1]
    bps = pps // ppb
    scale = HEAD_DIM ** -0.5 * LOG2E
    f32 = jnp.float32

    q = q_ref[...]
    q_rep = jnp.concatenate([q] * N_HEADS, axis=0)
    row_head = lax.broadcasted_iota(jnp.int32, (r, da), 0) // t
    lane_head = lax.broadcasted_iota(jnp.int32, (r, da), 1) // HEAD_DIM
    q_exp = jnp.where(row_head == lane_head, q_rep, 0.0)
    q_s = q_exp * scale
    q_log = q_s.astype(jnp.bfloat16)
    q_lo = (q_s - q_log.astype(f32)).astype(jnp.bfloat16)
    rg = (MXU_DIM // HEAD_DIM) * t
    groups = [(slice(a * rg, (a + 1) * rg), slice(a * MXU_DIM, (a + 1) * MXU_DIM)) for a in range(da // MXU_DIM)]
    q_two = [jnp.concatenate([q_log[rs, fs], q_lo[rs, fs]], axis=0) for rs, fs in groups]
    row_head_g = lax.broadcasted_iota(jnp.int32, (rg, MXU_DIM), 0) // t
    lane = lax.broadcasted_iota(jnp.int32, (r, LANES), 1)

    @pl.when(step == 0)
    def _():
        g_ref[...] = jnp.zeros(g_ref.shape, g_ref.dtype)
        m_ref[...] = jnp.full(m_ref.shape, NEG, m_ref.dtype)
        l_ref[...] = jnp.zeros(l_ref.shape, l_ref.dtype)

    g_new, m_new, l_new = g_ref[...], m_ref[...], l_ref[...]
    stats = {}

    def scores(bb):
        kb = jnp.concatenate([kp_refs[bb * ppb + p][0].astype(jnp.bfloat16) for p in range(ppb)], axis=1)
        s_two = [jnp.dot(q_two[a], kb[fs, :], preferred_element_type=f32)
                 for a, (_, fs) in enumerate(groups)]
        sb = jnp.concatenate([s[0:rg] for s in s_two], axis=0)
        gb = jnp.sum(sb + jnp.concatenate([s[rg:2 * rg] for s in s_two], axis=0),
                     axis=-1, keepdims=True)
        mb = jnp.max(sb, axis=-1, keepdims=True)
        pf = jnp.exp2(sb - mb)
        lb = jnp.sum(pf, axis=-1, keepdims=True)
        stats[bb] = (gb, mb, lb, pf.astype(jnp.bfloat16))

    def values(bb):
        n = step * bps + bb
        pb = stats[bb][3]
        vb = jnp.concatenate([vp_refs[bb * ppb + p][0].astype(jnp.bfloat16) for p in range(ppb)], axis=1)
        ob = [lax.dot_general(pb[rs], vb[fs, :], _NT, preferred_element_type=f32)
              for rs, fs in groups]
        op_ref[n] = jnp.concatenate([_fold_heads(o, row_head_g) for o in ob], axis=0)

    skew = 2
    for bb in range(bps + skew):
        if bb < bps:
            scores(bb)
        if bb >= skew:
            values(bb - skew)
    for bb in range(bps):
        n = step * bps + bb
        gb, mb, lb, _ = stats[bb]
        g_new = jnp.where(lane == n, gb, g_new)
        m_new = jnp.where(lane == n, mb, m_new)
        l_new = jnp.where(lane == n, lb, l_new)
    g_ref[...] = g_new
    m_ref[...] = m_new
    l_ref[...] = l_new

    @pl.when(step == pl.num_programs(1) - 1)
    def _():
        sc = jnp.where(lane < npb, g_ref[...], -jnp.inf)
        sel = jnp.zeros((r, LANES), jnp.bool_)
        for _ in range(min(MOBA_TOPK, npb)):
            best = jnp.max(sc, axis=-1, keepdims=True)
            first = jnp.min(jnp.where(sc == best, lane, LANES), axis=-1, keepdims=True)
            hit = lane == first
            sel = sel | hit
            sc = jnp.where(hit, -jnp.inf, sc)

        kn = kn_ref[...]
        vn = vn_ref[...]
        so = lax.dot_general(q_log, kn.astype(jnp.bfloat16), _NT, preferred_element_type=f32)
        tq = lax.broadcasted_iota(jnp.int32, (r, t), 0) % t
        tk = lax.broadcasted_iota(jnp.int32, (r, t), 1)
        so = jnp.where(tk <= tq, so, NEG)
        m_own = jnp.max(so, axis=-1, keepdims=True)
        p_own = jnp.exp2(so - m_own)
        l_own = jnp.sum(p_own, axis=-1, keepdims=True)
        o_own = _fold_heads(jnp.dot(p_own.astype(jnp.bfloat16), vn.astype(jnp.bfloat16),
                                    preferred_element_type=f32), row_head)

        m_sel = jnp.where(sel, m_ref[...], NEG)
        m_tot = jnp.maximum(m_own, jnp.max(m_sel, axis=-1, keepdims=True))
        w = jnp.where(sel, jnp.exp2(m_sel - m_tot), 0.0)
        w_own = jnp.exp2(m_own - m_tot)
        l_tot = w_own * l_own + jnp.sum(w * l_ref[...], axis=-1, keepdims=True)

        def add_block(nn, acc):
            wn = jnp.sum(jnp.where(lane == nn, w, 0.0), axis=-1, keepdims=True)
            return acc + wn * op_ref[nn]

        o_tot = lax.fori_loop(0, npb, add_block, w_own * o_own) / l_tot
        tiles = []
        for p in range(da // LANES):
            lo = o_tot[(2 * p) * t:(2 * p + 1) * t, :]
            hi = o_tot[(2 * p + 1) * t:(2 * p + 2) * t, :]
            tiles.append(jnp.where(lane[0:t] < HEAD_DIM, lo, hi))
        o_ref[...] = jnp.concatenate(tiles, axis=-1).astype(o_ref.dtype)


def _moba_sample(q, k_new, v_new, pool_k, pool_v, page_table, *, t):
    n, da = q.shape
    db, n_pages = page_table.shape
    n_phys, page = pool_k.shape[0], pool_k.shape[1]
    ppb = MOBA_BLOCK // page
    npb = (n_pages * page) // MOBA_BLOCK
    assert MOBA_BLOCK % page == 0 and n_pages == npb * ppb and npb >= 1 and npb <= LANES
    pps = PAGES_PER_STEP
    assert pps % ppb == 0 and n_pages % pps == 0 and t % SUBLANES == 0 and page % LANES == 0
    pk = jnp.transpose(pool_k, (0, 2, 3, 1)).reshape(n_phys, da, page)
    pv = jnp.transpose(pool_v, (0, 2, 3, 1)).reshape(n_phys, da, page)
    r = N_HEADS * t

    def page_spec(j):
        return pl.BlockSpec((1, da, page), lambda d, s, pt: (pt[d, s * pps + j], 0, 0))

    row_spec = pl.BlockSpec((t, da), lambda d, s, pt: (d, 0))
    grid_spec = pltpu.PrefetchScalarGridSpec(
        num_scalar_prefetch=1,
        grid=(db, n_pages // pps),
        in_specs=[row_spec, row_spec, row_spec] + [page_spec(j) for j in range(pps)] * 2,
        out_specs=row_spec,
        scratch_shapes=[pltpu.VMEM((r, LANES), jnp.float32),
                        pltpu.VMEM((r, LANES), jnp.float32),
                        pltpu.VMEM((r, LANES), jnp.float32),
                        pltpu.VMEM((npb, r, LANES), jnp.float32)],
    )
    return pl.pallas_call(
        functools.partial(_moba_sample_kernel, t=t, npb=npb, ppb=ppb, pps=pps),
        grid_spec=grid_spec,
        out_shape=jax.ShapeDtypeStruct((n, da), jnp.bfloat16),
        compiler_params=pltpu.CompilerParams(
            dimension_semantics=("arbitrary", "arbitrary"), vmem_limit_bytes=VMEM_LIMIT),
        name="moba_sample",
    )(page_table, q, k_new, v_new, *([pk] * pps), *([pv] * pps))


def _out_ffn_kernel(x_ref, a_ref, c_ref, hist_ref, wo_ref, gpm_ref, gpf_ref, wup_ref, wdw_ref, bdw_ref,
                    wdn_ref, gpo_ref, y_ref, st_ref, carry_ref, f_ref, *, gt, l, width):
    s = pl.program_id(1)
    d_ff = wdn_ref.shape[0]
    da = a_ref.shape[-1]
    pad = carry_ref.shape[1]
    h0 = pad - (width - 1)

    @pl.when(s == 0)
    def _():
        carry_ref[:, h0:pad, :] = hist_ref[...]

    mix = (jnp.dot(a_ref[...], wo_ref[0:da, :], preferred_element_type=jnp.float32)
           + jnp.dot(c_ref[...], wo_ref[da:, :], preferred_element_type=jnp.float32))
    x1 = x_ref[...] + _rms(mix, gpm_ref[...])
    h = _rms(x1, gpf_ref[...]).astype(jnp.bfloat16)

    def up_proj(lo):
        return jnp.dot(h, wup_ref[:, lo:lo + FF_CHUNK], preferred_element_type=jnp.float32)

    def conv(u, lo):
        hi = lo + FF_CHUNK
        u3 = u.reshape(gt, l, FF_CHUNK)
        ext = jnp.concatenate([carry_ref[:, :, lo:hi], u3], axis=1)
        carry_ref[:, :, lo:hi] = u3[:, l - pad:, :]
        acc = u3 * wdw_ref[width - 1:width, lo:hi]
        for j in range(width - 1):
            back = width - 1 - j
            acc = acc + pltpu.roll(ext, back, 1)[:, pad:, :] * wdw_ref[j:j + 1, lo:hi]
        return (acc + bdw_ref[:, lo:hi]).reshape(gt * l, FF_CHUNK)

    n_cc = d_ff // FF_CHUNK
    ups = {}
    for cc in range(n_cc + 1):
        if cc < n_cc:
            ups[cc] = (up_proj(cc * FF_CHUNK), up_proj(d_ff + cc * FF_CHUNK))
        if cc >= 1:
            ug, uv = ups.pop(cc - 1)
            gate = conv(ug, (cc - 1) * FF_CHUNK)
            val = conv(uv, d_ff + (cc - 1) * FF_CHUNK)
            f_ref[:, (cc - 1) * FF_CHUNK:cc * FF_CHUNK] = (_silu(gate) * val).astype(jnp.bfloat16)

    ff = jnp.dot(f_ref[...], wdn_ref[...], preferred_element_type=jnp.float32)
    y_ref[...] = x1 + _rms(ff, gpo_ref[...])

    @pl.when(s == pl.num_programs(1) - 1)
    def _():
        st_ref[...] = carry_ref[:, h0:pad, :]


def _out_ffn(x, attn, c, hist, wo, gpm, gpf, wup, wdw, bdw, wdn, gpo, *, gt, l):
    g_total, s_total, d = x.shape
    n = g_total * s_total
    d_ff = wdn.shape[0]
    width = wdw.shape[0]
    pad = SUBLANES
    assert width - 1 <= pad <= l and d_ff % FF_CHUNK == 0
    rows = gt * l
    n_s = s_total // l
    assert s_total % l == 0 and g_total % gt == 0 and l % SUBLANES == 0
    assert gt == 1 or n_s == 1
    row_spec = lambda cdim: pl.BlockSpec((rows, cdim), lambda g, s: (g * n_s + s, 0))
    st_spec = pl.BlockSpec((gt, width - 1, 2 * d_ff), lambda g, s: (g, 0, 0))
    f32 = jnp.float32
    return pl.pallas_call(
        functools.partial(_out_ffn_kernel, gt=gt, l=l, width=width),
        grid=(g_total // gt, n_s),
        in_specs=[
            row_spec(d), row_spec(attn.shape[1]), row_spec(c.shape[1]), st_spec,
            _const_spec(wo.shape), _const_spec((1, d)), _const_spec((1, d)),
            _const_spec(wup.shape), _const_spec(wdw.shape), _const_spec((1, 2 * d_ff)),
            _const_spec(wdn.shape), _const_spec((1, d)),
        ],
        out_specs=[row_spec(d), st_spec],
        out_shape=[jax.ShapeDtypeStruct((n, d), f32),
                   jax.ShapeDtypeStruct((g_total, width - 1, 2 * d_ff), f32)],
        scratch_shapes=[pltpu.VMEM((gt, pad, 2 * d_ff), f32),
                        pltpu.VMEM((rows, d_ff), jnp.bfloat16)],
        compiler_params=pltpu.CompilerParams(
            dimension_semantics=("arbitrary", "arbitrary"), vmem_limit_bytes=VMEM_LIMIT),
        name="out_ffn",
    )(x.reshape(n, d), attn, c, hist, wo, gpm.reshape(1, d), gpf.reshape(1, d), wup, wdw,
      bdw.reshape(1, -1), wdn, gpo.reshape(1, d))


def kernel(x_prompt, x_sample, cache_k, cache_v, page_table, state_cv, state_ffn, g_pre_mix, w_in, b_in,
           w_cv_dw, b_cv_dw, g_cv_ln, b_cv_ln, w_cv_pw, b_cv_pw, w_out, g_post_mix, g_pre_ffn, w_up,
           w_ffn_dw, b_ffn_dw, w_down, g_post_ffn):
    depth = w_in.shape[0]
    bf16 = jnp.bfloat16
    b, s_len, d = x_prompt.shape
    db, t, _ = x_sample.shape
    n_qkv = 3 * D_ATTN
    kv_s = (db, t, N_HEADS, HEAD_DIM)
    yp, ys = x_prompt, x_sample
    outs = [[] for _ in range(8)]
    for lyr in range(depth):
        w_qkv = w_in[lyr][:, :n_qkv].astype(bf16)
        b_qkv = b_in[lyr][:n_qkv]
        cv_prm = (w_in[lyr][:, n_qkv:].astype(bf16), b_in[lyr][n_qkv:], w_cv_dw[lyr], b_cv_dw[lyr],
                  g_cv_ln[lyr], b_cv_ln[lyr], w_cv_pw[lyr].astype(bf16), b_cv_pw[lyr])
        ffn_prm = (w_out[lyr].astype(bf16), g_post_mix[lyr], g_pre_ffn[lyr], w_up[lyr].astype(bf16),
                   w_ffn_dw[lyr], b_ffn_dw[lyr], w_down[lyr].astype(bf16), g_post_ffn[lyr])

        cv0 = jnp.zeros((b,) + state_cv.shape[2:], x_prompt.dtype)
        ff0 = jnp.zeros((b,) + state_ffn.shape[2:], x_prompt.dtype)
        lp = min(ROW_TILE, s_len)
        qt, kt, vt, c, cvp = _inproj_conv(yp, cv0, g_pre_mix[lyr], w_qkv.T, b_qkv.reshape(n_qkv, 1), *cv_prm,
                                          gt=1, l=lp, transposed=True)
        attn = _moba_prompt(qt, kt, vt).reshape(b * s_len, D_ATTN)
        yp, ffp = _out_ffn(yp, attn, c, ff0, *ffn_prm, gt=1, l=lp)
        yp = yp.reshape(b, s_len, d)
        kp = jnp.transpose(kt.reshape(b, N_HEADS, HEAD_DIM, s_len), (0, 3, 1, 2))
        vp = jnp.transpose(vt.reshape(b, N_HEADS, HEAD_DIM, s_len), (0, 3, 1, 2))

        q, k, v, c, cvs = _inproj_conv(ys, state_cv[lyr], g_pre_mix[lyr], w_qkv, b_qkv.reshape(1, n_qkv), *cv_prm,
                                       gt=db, l=t, transposed=False)
        attn = _moba_sample(q, k, v, cache_k[lyr], cache_v[lyr], page_table, t=t)
        ys, ffs = _out_ffn(ys, attn, c, state_ffn[lyr], *ffn_prm, gt=db, l=t)
        ys = ys.reshape(db, t, d)
        for lst, val in zip(outs, (kp, vp, cvp, ffp, k.reshape(kv_s), v.reshape(kv_s), cvs, ffs)):
            lst.append(val)
    return (yp, ys) + tuple(jnp.stack(lst) for lst in outs)
```

```python
import functools

import jax
import jax.numpy as jnp
from jax import lax
from jax.experimental import pallas as pl
from jax.experimental.pallas import tpu as pltpu

N_HEADS = 8
HEAD_DIM = 64
D_ATTN = N_HEADS * HEAD_DIM
MOBA_BLOCK = 256
MOBA_TOPK = 3
RMS_EPS = 1e-6
LN_EPS = 1e-5
NEG = -1e30
LOG2E = 1.4426950408889634

LANES = 128
SUBLANES = 8
MXU_DIM = 256
VMEM_LIMIT = 56 * 1024 * 1024

ROW_TILE = 512
CONV_CHUNK = 32
FF_CHUNK = 256
PASS_BLOCKS = 8
RING_PASSES = 3

_NT = (((1,), (1,)), ((), ()))
_HI = lax.Precision.HIGHEST


def _const_spec(shape):
    nd = len(shape)
    return pl.BlockSpec(shape, lambda *_: (0,) * nd, pipeline_mode=pl.Buffered(1))


def _rms(x, g):
    return x * lax.rsqrt(jnp.mean(x * x, axis=-1, keepdims=True) + RMS_EPS) * g


def _silu(x):
    return x / (1.0 + jnp.exp(-x))


def _inproj_conv_kernel(x_ref, hist_ref, gpre_ref, wqkv_ref, bqkv_ref, wcv_ref, bcv_ref, wdw_ref, bdw_ref,
                        gln_ref, bln_ref, wpw_ref, bpw_ref,
                        q_ref, k_ref, v_ref, c_ref, cvs_ref,
                        ext_ref, act_ref, *, gt, l, width, transposed):
    s = pl.program_id(1)
    dc = act_ref.shape[-1]
    pad = ext_ref.shape[1] - l
    h0 = pad - (width - 1)

    @pl.when(s == 0)
    def _():
        ext_ref[:, h0:pad, :] = hist_ref[...]

    h = _rms(x_ref[...], gpre_ref[...]).astype(jnp.bfloat16)

    cva = jnp.dot(h, wcv_ref[:, 0:dc], preferred_element_type=jnp.float32) + bcv_ref[:, 0:dc]
    cvg = jnp.dot(h, wcv_ref[:, dc:2 * dc], preferred_element_type=jnp.float32) + bcv_ref[:, dc:2 * dc]
    glu = cva / (1.0 + jnp.exp(-cvg))
    ext_ref[:, pad:pad + l, :] = glu.reshape(gt, l, dc)

    def qkv_proj(idx, ref):
        lo, hi = idx * D_ATTN, (idx + 1) * D_ATTN
        if transposed:
            ref[0] = lax.dot_general(wqkv_ref[lo:hi, :], h, _NT,
                                     preferred_element_type=jnp.float32) + bqkv_ref[lo:hi, :]
        else:
            ref[...] = jnp.dot(h, wqkv_ref[:, lo:hi], preferred_element_type=jnp.float32) + bqkv_ref[:, lo:hi]

    rc = min(l, CONV_CHUNK)
    n_chunks = l // rc
    chunks = [(g, cidx * rc) for g in range(gt) for cidx in range(n_chunks)]
    pending = [(len(chunks) * (idx + 1) // 4, idx, ref) for idx, ref in enumerate((q_ref, k_ref, v_ref))]
    for it in range(len(chunks) + 1):
        while pending and (it == len(chunks) or pending[0][0] <= it):
            _, idx, ref = pending.pop(0)
            qkv_proj(idx, ref)
        if it == len(chunks):
            break
        g, r0 = chunks[it]
        for lt in range(dc // LANES):
            ls = slice(lt * LANES, (lt + 1) * LANES)
            win = ext_ref[g, r0:r0 + rc + pad, ls]
            shifted = {r: pltpu.roll(win, rc + pad - r, 0) for r in range(1, SUBLANES)}
            acc = None
            for j in range(width):
                a, r = divmod(h0 + j, SUBLANES)
                lo = a * SUBLANES
                tap = ext_ref[g, r0 + lo:r0 + lo + rc, ls] if r == 0 else shifted[r][lo:lo + rc]
                term = tap * wdw_ref[j:j + 1, ls]
                acc = term if acc is None else acc + term
            act_ref[g * l + r0:g * l + r0 + rc, ls] = acc + bdw_ref[:, ls]
        cv = act_ref[g * l + r0:g * l + r0 + rc, :]
        mu = jnp.mean(cv, axis=-1, keepdims=True)
        cen = cv - mu
        var = jnp.mean(cen * cen, axis=-1, keepdims=True)
        y = cen * lax.rsqrt(var + LN_EPS) * gln_ref[...] + bln_ref[...]
        act_ref[g * l + r0:g * l + r0 + rc, :] = _silu(y)

    c = jnp.dot(act_ref[...].astype(jnp.bfloat16), wpw_ref[...],
                preferred_element_type=jnp.float32) + bpw_ref[...]
    c_ref[...] = c.astype(c_ref.dtype)

    @pl.when(s == pl.num_programs(1) - 1)
    def _():
        cvs_ref[...] = ext_ref[:, l + h0:l + pad, :]

    ext_ref[:, 0:pad, :] = ext_ref[:, l:l + pad, :]


def _inproj_conv(x, hist, gpre, wqkv, bqkv, wcv, bcv, wdw, bdw, gln, bln, wpw, bpw, *, gt, l, transposed):
    g_total, s_total, d = x.shape
    n = g_total * s_total
    dc = wpw.shape[0]
    width = wdw.shape[0]
    pad = -(-(width - 1) // SUBLANES) * SUBLANES
    rows = gt * l
    n_s = s_total // l
    assert s_total % l == 0 and g_total % gt == 0 and l % SUBLANES == 0
    assert gt == 1 or n_s == 1
    row_spec = lambda c: pl.BlockSpec((rows, c), lambda g, s: (g * n_s + s, 0))
    f32 = jnp.float32
    if transposed:
        assert gt == 1
        qkv_spec = pl.BlockSpec((1, D_ATTN, l), lambda g, s: (g, 0, s))
        qkv_shape = jax.ShapeDtypeStruct((g_total, D_ATTN, s_total), f32)
    else:
        qkv_spec = row_spec(D_ATTN)
        qkv_shape = jax.ShapeDtypeStruct((n, D_ATTN), f32)
    return pl.pallas_call(
        functools.partial(_inproj_conv_kernel, gt=gt, l=l, width=width, transposed=transposed),
        grid=(g_total // gt, n_s),
        in_specs=[
            row_spec(d),
            pl.BlockSpec((gt, width - 1, dc), lambda g, s: (g, 0, 0)),
            _const_spec((1, d)), _const_spec(wqkv.shape), _const_spec(bqkv.shape),
            _const_spec(wcv.shape), _const_spec((1, 2 * dc)),
            _const_spec(wdw.shape), _const_spec((1, dc)), _const_spec((1, dc)), _const_spec((1, dc)),
            _const_spec(wpw.shape), _const_spec((1, dc)),
        ],
        out_specs=[qkv_spec, qkv_spec, qkv_spec, row_spec(dc),
                   pl.BlockSpec((gt, width - 1, dc), lambda g, s: (g, 0, 0))],
        out_shape=[qkv_shape, qkv_shape, qkv_shape, jax.ShapeDtypeStruct((n, dc), jnp.bfloat16),
                   jax.ShapeDtypeStruct((g_total, width - 1, dc), f32)],
        scratch_shapes=[pltpu.VMEM((gt, pad + l, dc), f32), pltpu.VMEM((rows, dc), f32)],
        compiler_params=pltpu.CompilerParams(
            dimension_semantics=("arbitrary", "arbitrary"), vmem_limit_bytes=VMEM_LIMIT),
        name="inproj_conv",
    )(x.reshape(n, d), hist, gpre.reshape(1, d), wqkv, bqkv, wcv, bcv.reshape(1, 2 * dc), wdw,
      bdw.reshape(1, dc), gln.reshape(1, dc), bln.reshape(1, dc), wpw, bpw.reshape(1, dc))


def _moba_prompt_kernel(qt_ref, kt_ref, vt_ref, o_ref, kaug_ref, vaug_ref, s_ref, p_ref, *, nb):
    blk = MOBA_BLOCK
    s_len = kt_ref.shape[2]
    qscale = HEAD_DIM ** -0.5 * LOG2E
    f32 = jnp.float32
    n_buf = s_ref.shape[0]

    kt = kt_ref[0]
    vt = vt_ref[0]
    key_blk = lax.broadcasted_iota(jnp.int32, (nb, s_len), 1) // blk
    ind = jnp.where(key_blk == lax.broadcasted_iota(jnp.int32, (nb, s_len), 0), 1.0 / blk, 0.0)
    km = lax.dot_general(ind, kt, _NT, precision=_HI, preferred_element_type=f32)
    km_lane = lax.broadcasted_iota(jnp.int32, (nb, LANES), 1)

    k_rows = kt.T
    row_blk = lax.broadcasted_iota(jnp.int32, (s_len, LANES), 0) // blk
    lane_s = lax.broadcasted_iota(jnp.int32, (s_len, LANES), 1)
    sub_s = lax.broadcasted_iota(jnp.int32, (LANES, s_len), 0)
    for hh in range(2):
        own_lo = hh * HEAD_DIM
        base = HEAD_DIM - own_lo
        own_s = (lane_s >= own_lo) & (lane_s < own_lo + HEAD_DIM)
        onehot = (lane_s - base == row_blk).astype(f32)
        kaug_ref[hh] = jnp.where(own_s, k_rows, onehot).astype(jnp.bfloat16)
        own_r = (sub_s >= own_lo) & (sub_s < own_lo + HEAD_DIM)
        vaug_ref[hh] = jnp.where(own_r, vt, jnp.where(sub_s == base, 1.0, 0.0)).astype(jnp.bfloat16)

    sub = lax.broadcasted_iota(jnp.int32, (nb, blk), 0)
    rr = lax.broadcasted_iota(jnp.int32, (blk, blk), 0)
    cc = lax.broadcasted_iota(jnp.int32, (blk, blk), 1)
    causal = rr <= cc
    zeros_fill = jnp.zeros((HEAD_DIM - nb, blk), f32)
    items = [(i, hh) for i in range(nb) for hh in range(2)]
    o_heads = {}

    def logits(n):
        i, hh = items[n]
        w = (i + 1) * blk
        slot = n % n_buf
        own_lo = hh * HEAD_DIM
        qt_i = qt_ref[0, :, i * blk:(i + 1) * blk]
        km_h = jnp.where((km_lane >= own_lo) & (km_lane < own_lo + HEAD_DIM), km, 0.0)
        gate = jnp.dot(km_h, qt_i, precision=_HI, preferred_element_type=f32)
        cnt = jnp.zeros((nb, blk), jnp.int32)
        for m in range(i):
            row = gate[m:m + 1, :]
            beats = (row > gate) | ((row == gate) & (m < sub))
            cnt = cnt + beats.astype(jnp.int32)
        keep = ((sub < i) & (cnt < MOBA_TOPK)) | (sub == i)
        bias = jnp.where(keep, 0.0, NEG)
        q_own = qt_i[own_lo:own_lo + HEAD_DIM, :] * qscale
        parts = [q_own, bias, zeros_fill] if hh == 0 else [bias, zeros_fill, q_own]
        q_aug = jnp.concatenate(parts, axis=0).astype(jnp.bfloat16)
        if i > 0:
            s_ref[slot, 0:w - blk, :] = jnp.dot(kaug_ref[hh, 0:w - blk, :], q_aug, preferred_element_type=f32)
        s_own = jnp.dot(kaug_ref[hh, w - blk:w, :], q_aug, preferred_element_type=f32)
        s_ref[slot, w - blk:w, :] = jnp.where(causal, s_own, NEG)

    def softmax(n):
        i, _ = items[n]
        w = (i + 1) * blk
        slot = n % n_buf
        sv = s_ref[slot, 0:w, :]
        mx = jnp.max(sv, axis=0, keepdims=True)
        p_ref[slot, 0:w, :] = jnp.exp2(sv - mx).astype(jnp.bfloat16)

    def values(n):
        i, hh = items[n]
        w = (i + 1) * blk
        slot = n % n_buf
        own_lo = hh * HEAD_DIM
        base = HEAD_DIM - own_lo
        o_t = jnp.dot(vaug_ref[hh, :, 0:w], p_ref[slot, 0:w, :], preferred_element_type=f32)
        o_heads[hh] = o_t[own_lo:own_lo + HEAD_DIM, :] * (1.0 / o_t[base:base + 1, :])
        if hh == 1:
            o_pair = jnp.concatenate([o_heads[0], o_heads[1]], axis=0)
            o_ref[0, i * blk:(i + 1) * blk, :] = o_pair.T.astype(o_ref.dtype)

    lag = (n_buf - 1) // 2
    for n in range(len(items) + 2 * lag):
        if n < len(items):
            logits(n)
        if 0 <= n - lag < len(items):
            softmax(n - lag)
        if 0 <= n - 2 * lag < len(items):
            values(n - 2 * lag)


def _moba_prompt(qt, kt, vt):
    b, da, s_len = qt.shape
    assert s_len % MOBA_BLOCK == 0 and da % LANES == 0 and LANES == 2 * HEAD_DIM
    nb = s_len // MOBA_BLOCK
    assert nb <= HEAD_DIM
    n_buf = 5
    spec = pl.BlockSpec((1, LANES, s_len), lambda bi, hp: (bi, hp, 0))
    return pl.pallas_call(
        functools.partial(_moba_prompt_kernel, nb=nb),
        grid=(b, da // LANES),
        in_specs=[spec, spec, spec],
        out_specs=pl.BlockSpec((1, s_len, LANES), lambda bi, hp: (bi, 0, hp)),
        out_shape=jax.ShapeDtypeStruct((b, s_len, da), jnp.bfloat16),
        scratch_shapes=[pltpu.VMEM((2, s_len, LANES), jnp.bfloat16),
                        pltpu.VMEM((2, LANES, s_len), jnp.bfloat16),
                        pltpu.VMEM((n_buf, s_len, MOBA_BLOCK), jnp.float32),
                        pltpu.VMEM((n_buf, s_len, MOBA_BLOCK), jnp.bfloat16)],
        compiler_params=pltpu.CompilerParams(
            dimension_semantics=("arbitrary", "arbitrary"), vmem_limit_bytes=VMEM_LIMIT),
        name="moba_prompt",
    )(qt, kt, vt)


def _fold_heads(o, row_head):
    r, da = o.shape
    lane_head = lax.broadcasted_iota(jnp.int32, (r, da), 1) // HEAD_DIM
    om = jnp.where(lane_head == row_head, o, 0.0)
    out = om[:, 0:LANES]
    for p in range(1, da // LANES):
        out = out + om[:, p * LANES:(p + 1) * LANES]
    return out


def _moba_sample_kernel(pt_ref, q_ref, kn_ref, vn_ref, pk_ref, pv_ref, o_ref,
                        kbuf, vbuf, ksem, vsem, g_ref, m_ref, l_ref, op_ref, *, t, npb, ppb, pb):
    d = pl.program_id(0)
    r = N_HEADS * t
    da = q_ref.shape[-1]
    n_pass = npb // pb
    total_pass = pl.num_programs(0) * n_pass
    scale = HEAD_DIM ** -0.5 * LOG2E
    f32 = jnp.float32

    def pass_copies(gp, region):
        gp = jnp.minimum(gp, total_pass - 1)
        copies = []
        for bb in range(pb):
            g = gp * pb + bb
            slot = region * pb + bb
            for p in range(ppb):
                page = pt_ref[g // npb, (g % npb) * ppb + p]
                copies.append(pltpu.make_async_copy(pk_ref.at[page], kbuf.at[slot, p], ksem.at[slot, p]))
                copies.append(pltpu.make_async_copy(pv_ref.at[page], vbuf.at[slot, p], vsem.at[slot, p]))
        return copies

    @pl.when(d == 0)
    def _():
        for gp in range(RING_PASSES - 1):
            for c in pass_copies(gp, gp):
                c.start()

    q = q_ref[...]
    q_rep = jnp.concatenate([q] * N_HEADS, axis=0)
    row_head = lax.broadcasted_iota(jnp.int32, (r, da), 0) // t
    lane_head = lax.broadcasted_iota(jnp.int32, (r, da), 1) // HEAD_DIM
    q_exp = jnp.where(row_head == lane_head, q_rep, 0.0)
    q_s = q_exp * scale
    q_log = q_s.astype(jnp.bfloat16)
    q_lo = (q_s - q_log.astype(f32)).astype(jnp.bfloat16)
    rg = (MXU_DIM // HEAD_DIM) * t
    groups = [(slice(a * rg, (a + 1) * rg), slice(a * MXU_DIM, (a + 1) * MXU_DIM)) for a in range(da // MXU_DIM)]
    q_two = [jnp.concatenate([q_log[rs, fs], q_lo[rs, fs]], axis=0) for rs, fs in groups]
    row_head_g = lax.broadcasted_iota(jnp.int32, (rg, MXU_DIM), 0) // t
    lane = lax.broadcasted_iota(jnp.int32, (r, LANES), 1)

    g_ref[...] = jnp.zeros(g_ref.shape, g_ref.dtype)
    m_ref[...] = jnp.full(m_ref.shape, NEG, m_ref.dtype)
    l_ref[...] = jnp.zeros(l_ref.shape, l_ref.dtype)

    def one_pass(k, carry):
        n0 = k * pb
        gp = d * n_pass + k
        region = gp % RING_PASSES
        for c in pass_copies(gp, region):
            c.wait()
        for c in pass_copies(gp + RING_PASSES - 1, (gp + RING_PASSES - 1) % RING_PASSES):
            c.start()
        stats = {}

        def scores(bb):
            slot = region * pb + bb
            kb = jnp.concatenate([kbuf[slot, p].astype(jnp.bfloat16) for p in range(ppb)], axis=1)
            s_two = [jnp.dot(q_two[a], kb[fs, :], preferred_element_type=f32)
                     for a, (_, fs) in enumerate(groups)]
            sb = jnp.concatenate([s[0:rg] for s in s_two], axis=0)
            gb = jnp.sum(sb + jnp.concatenate([s[rg:2 * rg] for s in s_two], axis=0),
                         axis=-1, keepdims=True)
            mb = jnp.max(sb, axis=-1, keepdims=True)
            pf = jnp.exp2(sb - mb)
            lb = jnp.sum(pf, axis=-1, keepdims=True)
            stats[bb] = (gb, mb, lb, pf.astype(jnp.bfloat16))

        def values(bb):
            slot = region * pb + bb
            pbf = stats[bb][3]
            vb = jnp.concatenate([vbuf[slot, p].astype(jnp.bfloat16) for p in range(ppb)], axis=1)
            ob = [lax.dot_general(pbf[rs], vb[fs, :], _NT, preferred_element_type=f32)
                  for rs, fs in groups]
            op_ref[n0 + bb] = jnp.concatenate([_fold_heads(o, row_head_g) for o in ob], axis=0)

        skew = 2
        for bb in range(pb + skew):
            if bb < pb:
                scores(bb)
            if bb >= skew:
                values(bb - skew)
        g_new, m_new, l_new = g_ref[...], m_ref[...], l_ref[...]
        for bb in range(pb):
            gb, mb, lb, _ = stats[bb]
            g_new = jnp.where(lane == n0 + bb, gb, g_new)
            m_new = jnp.where(lane == n0 + bb, mb, m_new)
            l_new = jnp.where(lane == n0 + bb, lb, l_new)
        g_ref[...] = g_new
        m_ref[...] = m_new
        l_ref[...] = l_new
        return carry

    lax.fori_loop(0, n_pass, one_pass, 0)

    @pl.when(d == pl.num_programs(0) - 1)
    def _():
        for j in range(RING_PASSES - 1):
            for c in pass_copies(total_pass + j, (total_pass + j) % RING_PASSES):
                c.wait()

    sc = jnp.where(lane < npb, g_ref[...], -jnp.inf)
    sel = jnp.zeros((r, LANES), jnp.bool_)
    for _ in range(min(MOBA_TOPK, npb)):
        best = jnp.max(sc, axis=-1, keepdims=True)
        first = jnp.min(jnp.where(sc == best, lane, LANES), axis=-1, keepdims=True)
        hit = lane == first
        sel = sel | hit
        sc = jnp.where(hit, -jnp.inf, sc)

    kn = kn_ref[...]
    vn = vn_ref[...]
    so = lax.dot_general(q_log, kn.astype(jnp.bfloat16), _NT, preferred_element_type=f32)
    tq = lax.broadcasted_iota(jnp.int32, (r, t), 0) % t
    tk = lax.broadcasted_iota(jnp.int32, (r, t), 1)
    so = jnp.where(tk <= tq, so, NEG)
    m_own = jnp.max(so, axis=-1, keepdims=True)
    p_own = jnp.exp2(so - m_own)
    l_own = jnp.sum(p_own, axis=-1, keepdims=True)
    o_own = _fold_heads(jnp.dot(p_own.astype(jnp.bfloat16), vn.astype(jnp.bfloat16),
                                preferred_element_type=f32), row_head)

    m_sel = jnp.where(sel, m_ref[...], NEG)
    m_tot = jnp.maximum(m_own, jnp.max(m_sel, axis=-1, keepdims=True))
    w = jnp.where(sel, jnp.exp2(m_sel - m_tot), 0.0)
    w_own = jnp.exp2(m_own - m_tot)
    l_tot = w_own * l_own + jnp.sum(w * l_ref[...], axis=-1, keepdims=True)

    def add_block(nn, acc):
        wn = jnp.sum(jnp.where(lane == nn, w, 0.0), axis=-1, keepdims=True)
        return acc + wn * op_ref[nn]

    o_tot = lax.fori_loop(0, npb, add_block, w_own * o_own) / l_tot
    tiles = []
    for p in range(da // LANES):
        lo = o_tot[(2 * p) * t:(2 * p + 1) * t, :]
        hi = o_tot[(2 * p + 1) * t:(2 * p + 2) * t, :]
        tiles.append(jnp.where(lane[0:t] < HEAD_DIM, lo, hi))
    o_ref[...] = jnp.concatenate(tiles, axis=-1).astype(o_ref.dtype)


def _moba_sample(q, k_new, v_new, pool_k, pool_v, page_table, *, t):
    n, da = q.shape
    db, n_pages = page_table.shape
    n_phys, page = pool_k.shape[0], pool_k.shape[1]
    ppb = MOBA_BLOCK // page
    npb = (n_pages * page) // MOBA_BLOCK
    assert MOBA_BLOCK % page == 0 and n_pages == npb * ppb and npb >= 1 and npb <= LANES
    pb = PASS_BLOCKS
    slots = RING_PASSES * pb
    assert npb % pb == 0 and db * (npb // pb) >= RING_PASSES and t % SUBLANES == 0 and page % LANES == 0
    pk = jnp.transpose(pool_k, (0, 2, 3, 1)).reshape(n_phys, da, page)
    pv = jnp.transpose(pool_v, (0, 2, 3, 1)).reshape(n_phys, da, page)
    r = N_HEADS * t
    row_spec = pl.BlockSpec((t, da), lambda d, pt: (d, 0))
    hbm_spec = pl.BlockSpec(memory_space=pl.ANY)
    grid_spec = pltpu.PrefetchScalarGridSpec(
        num_scalar_prefetch=1,
        grid=(db,),
        in_specs=[row_spec, row_spec, row_spec, hbm_spec, hbm_spec],
        out_specs=row_spec,
        scratch_shapes=[pltpu.VMEM((slots, ppb, da, page), jnp.float32),
                        pltpu.VMEM((slots, ppb, da, page), jnp.float32),
                        pltpu.SemaphoreType.DMA((slots, ppb)),
                        pltpu.SemaphoreType.DMA((slots, ppb)),
                        pltpu.VMEM((r, LANES), jnp.float32),
                        pltpu.VMEM((r, LANES), jnp.float32),
                        pltpu.VMEM((r, LANES), jnp.float32),
                        pltpu.VMEM((npb, r, LANES), jnp.float32)],
    )
    return pl.pallas_call(
        functools.partial(_moba_sample_kernel, t=t, npb=npb, ppb=ppb, pb=pb),
        grid_spec=grid_spec,
        out_shape=jax.ShapeDtypeStruct((n, da), jnp.bfloat16),
        compiler_params=pltpu.CompilerParams(
            dimension_semantics=("arbitrary",), vmem_limit_bytes=VMEM_LIMIT),
        name="moba_sample",
    )(page_table, q, k_new, v_new, pk, pv)


def _out_ffn_kernel(x_ref, a_ref, c_ref, hist_ref, wo_ref, gpm_ref, gpf_ref, wup_ref, wdw_ref, bdw_ref,
                    wdn_ref, gpo_ref, y_ref, st_ref, carry_ref, f_ref, *, gt, l, width):
    s = pl.program_id(1)
    d_ff = wdn_ref.shape[0]
    da = a_ref.shape[-1]
    pad = carry_ref.shape[1]
    h0 = pad - (width - 1)

    @pl.when(s == 0)
    def _():
        carry_ref[:, h0:pad, :] = hist_ref[...]

    mix = (jnp.dot(a_ref[...], wo_ref[0:da, :], preferred_element_type=jnp.float32)
           + jnp.dot(c_ref[...], wo_ref[da:, :], preferred_element_type=jnp.float32))
    x1 = x_ref[...] + _rms(mix, gpm_ref[...])
    h = _rms(x1, gpf_ref[...]).astype(jnp.bfloat16)

    def up_proj(lo):
        return jnp.dot(h, wup_ref[:, lo:lo + FF_CHUNK], preferred_element_type=jnp.float32)

    def conv(u, lo):
        hi = lo + FF_CHUNK
        u3 = u.reshape(gt, l, FF_CHUNK)
        ext = jnp.concatenate([carry_ref[:, :, lo:hi], u3], axis=1)
        carry_ref[:, :, lo:hi] = u3[:, l - pad:, :]
        acc = u3 * wdw_ref[width - 1:width, lo:hi]
        for j in range(width - 1):
            back = width - 1 - j
            acc = acc + pltpu.roll(ext, back, 1)[:, pad:, :] * wdw_ref[j:j + 1, lo:hi]
        return (acc + bdw_ref[:, lo:hi]).reshape(gt * l, FF_CHUNK)

    n_cc = d_ff // FF_CHUNK
    ups = {}
    for cc in range(n_cc + 1):
        if cc < n_cc:
            ups[cc] = (up_proj(cc * FF_CHUNK), up_proj(d_ff + cc * FF_CHUNK))
        if cc >= 1:
            ug, uv = ups.pop(cc - 1)
            gate = conv(ug, (cc - 1) * FF_CHUNK)
            val = conv(uv, d_ff + (cc - 1) * FF_CHUNK)
            f_ref[:, (cc - 1) * FF_CHUNK:cc * FF_CHUNK] = (_silu(gate) * val).astype(jnp.bfloat16)

    ff = jnp.dot(f_ref[...], wdn_ref[...], preferred_element_type=jnp.float32)
    y_ref[...] = x1 + _rms(ff, gpo_ref[...])

    @pl.when(s == pl.num_programs(1) - 1)
    def _():
        st_ref[...] = carry_ref[:, h0:pad, :]


def _out_ffn(x, attn, c, hist, wo, gpm, gpf, wup, wdw, bdw, wdn, gpo, *, gt, l):
    g_total, s_total, d = x.shape
    n = g_total * s_total
    d_ff = wdn.shape[0]
    width = wdw.shape[0]
    pad = SUBLANES
    assert width - 1 <= pad <= l and d_ff % FF_CHUNK == 0
    rows = gt * l
    n_s = s_total // l
    assert s_total % l == 0 and g_total % gt == 0 and l % SUBLANES == 0
    assert gt == 1 or n_s == 1
    row_spec = lambda cdim: pl.BlockSpec((rows, cdim), lambda g, s: (g * n_s + s, 0))
    st_spec = pl.BlockSpec((gt, width - 1, 2 * d_ff), lambda g, s: (g, 0, 0))
    f32 = jnp.float32
    return pl.pallas_call(
        functools.partial(_out_ffn_kernel, gt=gt, l=l, width=width),
        grid=(g_total // gt, n_s),
        in_specs=[
            row_spec(d), row_spec(attn.shape[1]), row_spec(c.shape[1]), st_spec,
            _const_spec(wo.shape), _const_spec((1, d)), _const_spec((1, d)),
            _const_spec(wup.shape), _const_spec(wdw.shape), _const_spec((1, 2 * d_ff)),
            _const_spec(wdn.shape), _const_spec((1, d)),
        ],
        out_specs=[row_spec(d), st_spec],
        out_shape=[jax.ShapeDtypeStruct((n, d), f32),
                   jax.ShapeDtypeStruct((g_total, width - 1, 2 * d_ff), f32)],
        scratch_shapes=[pltpu.VMEM((gt, pad, 2 * d_ff), f32),
                        pltpu.VMEM((rows, d_ff), jnp.bfloat16)],
        compiler_params=pltpu.CompilerParams(
            dimension_semantics=("arbitrary", "arbitrary"), vmem_limit_bytes=VMEM_LIMIT),
        name="out_ffn",
    )(x.reshape(n, d), attn, c, hist, wo, gpm.reshape(1, d), gpf.reshape(1, d), wup, wdw,
      bdw.reshape(1, -1), wdn, gpo.reshape(1, d))


def kernel(x_prompt, x_sample, cache_k, cache_v, page_table, state_cv, state_ffn, g_pre_mix, w_in, b_in,
           w_cv_dw, b_cv_dw, g_cv_ln, b_cv_ln, w_cv_pw, b_cv_pw, w_out, g_post_mix, g_pre_ffn, w_up,
           w_ffn_dw, b_ffn_dw, w_down, g_post_ffn):
    depth = w_in.shape[0]
    bf16 = jnp.bfloat16
    b, s_len, d = x_prompt.shape
    db, t, _ = x_sample.shape
    n_qkv = 3 * D_ATTN
    kv_s = (db, t, N_HEADS, HEAD_DIM)
    yp, ys = x_prompt, x_sample
    outs = [[] for _ in range(8)]
    for lyr in range(depth):
        w_qkv = w_in[lyr][:, :n_qkv].astype(bf16)
        b_qkv = b_in[lyr][:n_qkv]
        cv_prm = (w_in[lyr][:, n_qkv:].astype(bf16), b_in[lyr][n_qkv:], w_cv_dw[lyr], b_cv_dw[lyr],
                  g_cv_ln[lyr], b_cv_ln[lyr], w_cv_pw[lyr].astype(bf16), b_cv_pw[lyr])
        ffn_prm = (w_out[lyr].astype(bf16), g_post_mix[lyr], g_pre_ffn[lyr], w_up[lyr].astype(bf16),
                   w_ffn_dw[lyr], b_ffn_dw[lyr], w_down[lyr].astype(bf16), g_post_ffn[lyr])

        cv0 = jnp.zeros((b,) + state_cv.shape[2:], x_prompt.dtype)
        ff0 = jnp.zeros((b,) + state_ffn.shape[2:], x_prompt.dtype)
        lp = min(ROW_TILE, s_len)
        qt, kt, vt, c, cvp = _inproj_conv(yp, cv0, g_pre_mix[lyr], w_qkv.T, b_qkv.reshape(n_qkv, 1), *cv_prm,
                                          gt=1, l=lp, transposed=True)
        attn = _moba_prompt(qt, kt, vt).reshape(b * s_len, D_ATTN)
        yp, ffp = _out_ffn(yp, attn, c, ff0, *ffn_prm, gt=1, l=lp)
        yp = yp.reshape(b, s_len, d)
        kp = jnp.transpose(kt.reshape(b, N_HEADS, HEAD_DIM, s_len), (0, 3, 1, 2))
        vp = jnp.transpose(vt.reshape(b, N_HEADS, HEAD_DIM, s_len), (0, 3, 1, 2))

        q, k, v, c, cvs = _inproj_conv(ys, state_cv[lyr], g_pre_mix[lyr], w_qkv, b_qkv.reshape(1, n_qkv), *cv_prm,
                                       gt=db, l=t, transposed=False)
        attn = _moba_sample(q, k, v, cache_k[lyr], cache_v[lyr], page_table, t=t)
        ys, ffs = _out_ffn(ys, attn, c, state_ffn[lyr], *ffn_prm, gt=db, l=t)
        ys = ys.reshape(db, t, d)
        for lst, val in zip(outs, (kp, vp, cvp, ffp, k.reshape(kv_s), v.reshape(kv_s), cvs, ffs)):
            lst.append(val)
    return (yp, ys) + tuple(jnp.stack(lst) for lst in outs)
```

```python
import functools

import jax
import jax.numpy as jnp
from jax import lax
from jax.experimental import pallas as pl
from jax.experimental.pallas import tpu as pltpu

N_HEADS = 8
HEAD_DIM = 64
D_ATTN = N_HEADS * HEAD_DIM
MOBA_BLOCK = 256
MOBA_TOPK = 3
RMS_EPS = 1e-6
LN_EPS = 1e-5
NEG = -1e30
LOG2E = 1.4426950408889634

LANES = 128
SUBLANES = 8
MXU_DIM = 256
VMEM_LIMIT = 56 * 1024 * 1024

ROW_TILE = 512
CONV_CHUNK = 32
FF_CHUNK = 256
PASS_BLOCKS = 8
RING_PASSES = 3

_NT = (((1,), (1,)), ((), ()))
_HI = lax.Precision.HIGHEST


def _const_spec(shape):
    nd = len(shape)
    return pl.BlockSpec(shape, lambda *_: (0,) * nd, pipeline_mode=pl.Buffered(1))


def _rms(x, g):
    return x * lax.rsqrt(jnp.mean(x * x, axis=-1, keepdims=True) + RMS_EPS) * g


def _silu(x):
    return x / (1.0 + jnp.exp(-x))


def _inproj_conv_kernel(x_ref, hist_ref, gpre_ref, wqkv_ref, bqkv_ref, wcv_ref, bcv_ref, wdw_ref, bdw_ref,
                        gln_ref, bln_ref, wpw_ref, bpw_ref,
                        q_ref, k_ref, v_ref, c_ref, cvs_ref,
                        ext_ref, act_ref, *, gt, l, width, transposed):
    s = pl.program_id(1)
    dc = act_ref.shape[-1]
    pad = ext_ref.shape[1] - l
    h0 = pad - (width - 1)

    @pl.when(s == 0)
    def _():
        ext_ref[:, h0:pad, :] = hist_ref[...]

    h = _rms(x_ref[...], gpre_ref[...]).astype(jnp.bfloat16)

    cva = jnp.dot(h, wcv_ref[:, 0:dc], preferred_element_type=jnp.float32) + bcv_ref[:, 0:dc]
    cvg = jnp.dot(h, wcv_ref[:, dc:2 * dc], preferred_element_type=jnp.float32) + bcv_ref[:, dc:2 * dc]
    glu = cva / (1.0 + jnp.exp(-cvg))
    ext_ref[:, pad:pad + l, :] = glu.reshape(gt, l, dc)

    def qkv_proj(idx, ref):
        lo, hi = idx * D_ATTN, (idx + 1) * D_ATTN
        if transposed:
            ref[0] = lax.dot_general(wqkv_ref[lo:hi, :], h, _NT,
                                     preferred_element_type=jnp.float32) + bqkv_ref[lo:hi, :]
        else:
            ref[...] = jnp.dot(h, wqkv_ref[:, lo:hi], preferred_element_type=jnp.float32) + bqkv_ref[:, lo:hi]

    rc = min(l, CONV_CHUNK)
    n_chunks = l // rc
    chunks = [(g, cidx * rc) for g in range(gt) for cidx in range(n_chunks)]
    pending = [(len(chunks) * (idx + 1) // 4, idx, ref) for idx, ref in enumerate((q_ref, k_ref, v_ref))]
    for it in range(len(chunks) + 1):
        while pending and (it == len(chunks) or pending[0][0] <= it):
            _, idx, ref = pending.pop(0)
            qkv_proj(idx, ref)
        if it == len(chunks):
            break
        g, r0 = chunks[it]
        for lt in range(dc // LANES):
            ls = slice(lt * LANES, (lt + 1) * LANES)
            win = ext_ref[g, r0:r0 + rc + pad, ls]
            shifted = {r: pltpu.roll(win, rc + pad - r, 0) for r in range(1, SUBLANES)}
            acc = None
            for j in range(width):
                a, r = divmod(h0 + j, SUBLANES)
                lo = a * SUBLANES
                tap = ext_ref[g, r0 + lo:r0 + lo + rc, ls] if r == 0 else shifted[r][lo:lo + rc]
                term = tap * wdw_ref[j:j + 1, ls]
                acc = term if acc is None else acc + term
            act_ref[g * l + r0:g * l + r0 + rc, ls] = acc + bdw_ref[:, ls]
        cv = act_ref[g * l + r0:g * l + r0 + rc, :]
        mu = jnp.mean(cv, axis=-1, keepdims=True)
        cen = cv - mu
        var = jnp.mean(cen * cen, axis=-1, keepdims=True)
        y = cen * lax.rsqrt(var + LN_EPS) * gln_ref[...] + bln_ref[...]
        act_ref[g * l + r0:g * l + r0 + rc, :] = _silu(y)

    c = jnp.dot(act_ref[...].astype(jnp.bfloat16), wpw_ref[...],
                preferred_element_type=jnp.float32) + bpw_ref[...]
    c_ref[...] = c.astype(c_ref.dtype)

    @pl.when(s == pl.num_programs(1) - 1)
    def _():
        cvs_ref[...] = ext_ref[:, l + h0:l + pad, :]

    ext_ref[:, 0:pad, :] = ext_ref[:, l:l + pad, :]


def _inproj_conv(x, hist, gpre, wqkv, bqkv, wcv, bcv, wdw, bdw, gln, bln, wpw, bpw, *, gt, l, transposed):
    g_total, s_total, d = x.shape
    n = g_total * s_total
    dc = wpw.shape[0]
    width = wdw.shape[0]
    pad = -(-(width - 1) // SUBLANES) * SUBLANES
    rows = gt * l
    n_s = s_total // l
    assert s_total % l == 0 and g_total % gt == 0 and l % SUBLANES == 0
    assert gt == 1 or n_s == 1
    row_spec = lambda c: pl.BlockSpec((rows, c), lambda g, s: (g * n_s + s, 0))
    f32 = jnp.float32
    if transposed:
        assert gt == 1
        qkv_spec = pl.BlockSpec((1, D_ATTN, l), lambda g, s: (g, 0, s))
        qkv_shape = jax.ShapeDtypeStruct((g_total, D_ATTN, s_total), f32)
    else:
        qkv_spec = row_spec(D_ATTN)
        qkv_shape = jax.ShapeDtypeStruct((n, D_ATTN), f32)
    return pl.pallas_call(
        functools.partial(_inproj_conv_kernel, gt=gt, l=l, width=width, transposed=transposed),
        grid=(g_total // gt, n_s),
        in_specs=[
            row_spec(d),
            pl.BlockSpec((gt, width - 1, dc), lambda g, s: (g, 0, 0)),
            _const_spec((1, d)), _const_spec(wqkv.shape), _const_spec(bqkv.shape),
            _const_spec(wcv.shape), _const_spec((1, 2 * dc)),
            _const_spec(wdw.shape), _const_spec((1, dc)), _const_spec((1, dc)), _const_spec((1, dc)),
            _const_spec(wpw.shape), _const_spec((1, dc)),
        ],
        out_specs=[qkv_spec, qkv_spec, qkv_spec, row_spec(dc),
                   pl.BlockSpec((gt, width - 1, dc), lambda g, s: (g, 0, 0))],
        out_shape=[qkv_shape, qkv_shape, qkv_shape, jax.ShapeDtypeStruct((n, dc), jnp.bfloat16),
                   jax.ShapeDtypeStruct((g_total, width - 1, dc), f32)],
        scratch_shapes=[pltpu.VMEM((gt, pad + l, dc), f32), pltpu.VMEM((rows, dc), f32)],
        compiler_params=pltpu.CompilerParams(
            dimension_semantics=("arbitrary", "arbitrary"), vmem_limit_bytes=VMEM_LIMIT),
        name="inproj_conv",
    )(x.reshape(n, d), hist, gpre.reshape(1, d), wqkv, bqkv, wcv, bcv.reshape(1, 2 * dc), wdw,
      bdw.reshape(1, dc), gln.reshape(1, dc), bln.reshape(1, dc), wpw, bpw.reshape(1, dc))


def _moba_prompt_kernel(qt_ref, kt_ref, vt_ref, o_ref, kaug_ref, vaug_ref, s_ref, p_ref, *, nb):
    blk = MOBA_BLOCK
    s_len = kt_ref.shape[2]
    qscale = HEAD_DIM ** -0.5 * LOG2E
    f32 = jnp.float32
    n_buf = s_ref.shape[0]

    kt = kt_ref[0]
    vt = vt_ref[0]
    key_blk = lax.broadcasted_iota(jnp.int32, (nb, s_len), 1) // blk
    ind = jnp.where(key_blk == lax.broadcasted_iota(jnp.int32, (nb, s_len), 0), 1.0 / blk, 0.0)
    km = lax.dot_general(ind, kt, _NT, precision=_HI, preferred_element_type=f32)
    km_lane = lax.broadcasted_iota(jnp.int32, (nb, LANES), 1)

    k_rows = kt.T
    row_blk = lax.broadcasted_iota(jnp.int32, (s_len, LANES), 0) // blk
    lane_s = lax.broadcasted_iota(jnp.int32, (s_len, LANES), 1)
    sub_s = lax.broadcasted_iota(jnp.int32, (LANES, s_len), 0)
    for hh in range(2):
        own_lo = hh * HEAD_DIM
        base = HEAD_DIM - own_lo
        own_s = (lane_s >= own_lo) & (lane_s < own_lo + HEAD_DIM)
        onehot = (lane_s - base == row_blk).astype(f32)
        kaug_ref[hh] = jnp.where(own_s, k_rows, onehot).astype(jnp.bfloat16)
        own_r = (sub_s >= own_lo) & (sub_s < own_lo + HEAD_DIM)
        vaug_ref[hh] = jnp.where(own_r, vt, jnp.where(sub_s == base, 1.0, 0.0)).astype(jnp.bfloat16)

    sub = lax.broadcasted_iota(jnp.int32, (nb, blk), 0)
    rr = lax.broadcasted_iota(jnp.int32, (blk, blk), 0)
    cc = lax.broadcasted_iota(jnp.int32, (blk, blk), 1)
    causal = rr <= cc
    zeros_fill = jnp.zeros((HEAD_DIM - nb, blk), f32)
    items = [(i, hh) for i in range(nb) for hh in range(2)]
    km_both = jnp.concatenate([jnp.where((km_lane >= hh * HEAD_DIM) & (km_lane < (hh + 1) * HEAD_DIM), km, 0.0)
                               for hh in range(2)], axis=0)
    gates = {}
    o_heads = {}

    def logits(n):
        i, hh = items[n]
        w = (i + 1) * blk
        slot = n % n_buf
        own_lo = hh * HEAD_DIM
        qt_i = qt_ref[0, :, i * blk:(i + 1) * blk]
        if hh == 0:
            gates[i] = jnp.dot(km_both, qt_i, precision=_HI, preferred_element_type=f32)
        gate = gates[i][hh * nb:(hh + 1) * nb]
        cnt = jnp.zeros((nb, blk), jnp.int32)
        for m in range(i):
            row = gate[m:m + 1, :]
            beats = (row > gate) | ((row == gate) & (m < sub))
            cnt = cnt + beats.astype(jnp.int32)
        keep = ((sub < i) & (cnt < MOBA_TOPK)) | (sub == i)
        bias = jnp.where(keep, 0.0, NEG)
        q_own = qt_i[own_lo:own_lo + HEAD_DIM, :] * qscale
        parts = [q_own, bias, zeros_fill] if hh == 0 else [bias, zeros_fill, q_own]
        q_aug = jnp.concatenate(parts, axis=0).astype(jnp.bfloat16)
        if i > 0:
            s_ref[slot, 0:w - blk, :] = jnp.dot(kaug_ref[hh, 0:w - blk, :], q_aug, preferred_element_type=f32)
        s_own = jnp.dot(kaug_ref[hh, w - blk:w, :], q_aug, preferred_element_type=f32)
        s_ref[slot, w - blk:w, :] = jnp.where(causal, s_own, NEG)

    def softmax(n):
        i, _ = items[n]
        w = (i + 1) * blk
        slot = n % n_buf
        sv = s_ref[slot, 0:w, :]
        mx = jnp.max(sv, axis=0, keepdims=True)
        p_ref[slot, 0:w, :] = jnp.exp2(sv - mx).astype(jnp.bfloat16)

    def values(n):
        i, hh = items[n]
        w = (i + 1) * blk
        slot = n % n_buf
        own_lo = hh * HEAD_DIM
        base = HEAD_DIM - own_lo
        o_t = jnp.dot(vaug_ref[hh, :, 0:w], p_ref[slot, 0:w, :], preferred_element_type=f32)
        o_heads[hh] = o_t[own_lo:own_lo + HEAD_DIM, :] * (1.0 / o_t[base:base + 1, :])
        if hh == 1:
            o_pair = jnp.concatenate([o_heads[0], o_heads[1]], axis=0)
            o_ref[0, i * blk:(i + 1) * blk, :] = o_pair.T.astype(o_ref.dtype)

    lag = (n_buf - 1) // 2
    for n in range(len(items) + 2 * lag):
        if n < len(items):
            logits(n)
        if 0 <= n - lag < len(items):
            softmax(n - lag)
        if 0 <= n - 2 * lag < len(items):
            values(n - 2 * lag)


def _moba_prompt(qt, kt, vt):
    b, da, s_len = qt.shape
    assert s_len % MOBA_BLOCK == 0 and da % LANES == 0 and LANES == 2 * HEAD_DIM
    nb = s_len // MOBA_BLOCK
    assert nb <= HEAD_DIM
    n_buf = 5
    spec = pl.BlockSpec((1, LANES, s_len), lambda bi, hp: (bi, hp, 0))
    return pl.pallas_call(
        functools.partial(_moba_prompt_kernel, nb=nb),
        grid=(b, da // LANES),
        in_specs=[spec, spec, spec],
        out_specs=pl.BlockSpec((1, s_len, LANES), lambda bi, hp: (bi, 0, hp)),
        out_shape=jax.ShapeDtypeStruct((b, s_len, da), jnp.bfloat16),
        scratch_shapes=[pltpu.VMEM((2, s_len, LANES), jnp.bfloat16),
                        pltpu.VMEM((2, LANES, s_len), jnp.bfloat16),
                        pltpu.VMEM((n_buf, s_len, MOBA_BLOCK), jnp.float32),
                        pltpu.VMEM((n_buf, s_len, MOBA_BLOCK), jnp.bfloat16)],
        compiler_params=pltpu.CompilerParams(
            dimension_semantics=("arbitrary", "arbitrary"), vmem_limit_bytes=VMEM_LIMIT),
        name="moba_prompt",
    )(qt, kt, vt)


def _fold_heads(o, row_head):
    r, da = o.shape
    lane_head = lax.broadcasted_iota(jnp.int32, (r, da), 1) // HEAD_DIM
    om = jnp.where(lane_head == row_head, o, 0.0)
    out = om[:, 0:LANES]
    for p in range(1, da // LANES):
        out = out + om[:, p * LANES:(p + 1) * LANES]
    return out


def _moba_sample_kernel(pt_ref, q_ref, kn_ref, vn_ref, pk_ref, pv_ref, o_ref,
                        kbuf, vbuf, ksem, vsem, g_ref, m_ref, l_ref, op_ref, *, t, npb, ppb, pb):
    d = pl.program_id(0)
    r = N_HEADS * t
    da = q_ref.shape[-1]
    n_pass = npb // pb
    total_pass = pl.num_programs(0) * n_pass
    scale = HEAD_DIM ** -0.5 * LOG2E
    f32 = jnp.float32

    def pass_copies(gp, region):
        gp = jnp.minimum(gp, total_pass - 1)
        copies = []
        for bb in range(pb):
            g = gp * pb + bb
            slot = region * pb + bb
            for p in range(ppb):
                page = pt_ref[g // npb, (g % npb) * ppb + p]
                copies.append(pltpu.make_async_copy(pk_ref.at[page], kbuf.at[slot, p], ksem.at[slot, p]))
                copies.append(pltpu.make_async_copy(pv_ref.at[page], vbuf.at[slot, p], vsem.at[slot, p]))
        return copies

    def start_all(copies):
        for idx, c in enumerate(copies):
            c.start(priority=idx % 2)

    @pl.when(d == 0)
    def _():
        for gp in range(RING_PASSES - 1):
            start_all(pass_copies(gp, gp))

    q = q_ref[...]
    q_rep = jnp.concatenate([q] * N_HEADS, axis=0)
    row_head = lax.broadcasted_iota(jnp.int32, (r, da), 0) // t
    lane_head = lax.broadcasted_iota(jnp.int32, (r, da), 1) // HEAD_DIM
    q_exp = jnp.where(row_head == lane_head, q_rep, 0.0)
    q_s = q_exp * scale
    q_log = q_s.astype(jnp.bfloat16)
    q_lo = (q_s - q_log.astype(f32)).astype(jnp.bfloat16)
    rg = (MXU_DIM // HEAD_DIM) * t
    groups = [(slice(a * rg, (a + 1) * rg), slice(a * MXU_DIM, (a + 1) * MXU_DIM)) for a in range(da // MXU_DIM)]
    q_two = [jnp.concatenate([q_log[rs, fs], q_lo[rs, fs]], axis=0) for rs, fs in groups]
    row_head_g = lax.broadcasted_iota(jnp.int32, (rg, MXU_DIM), 0) // t
    lane = lax.broadcasted_iota(jnp.int32, (r, LANES), 1)

    g_ref[...] = jnp.zeros(g_ref.shape, g_ref.dtype)
    m_ref[...] = jnp.full(m_ref.shape, NEG, m_ref.dtype)
    l_ref[...] = jnp.zeros(l_ref.shape, l_ref.dtype)

    def one_pass(k, carry):
        n0 = k * pb
        gp = d * n_pass + k
        region = gp % RING_PASSES
        for c in pass_copies(gp, region):
            c.wait()
        start_all(pass_copies(gp + RING_PASSES - 1, (gp + RING_PASSES - 1) % RING_PASSES))
        stats = {}

        def scores(bb):
            slot = region * pb + bb
            kb = jnp.concatenate([kbuf[slot, p].astype(jnp.bfloat16) for p in range(ppb)], axis=1)
            s_two = [jnp.dot(q_two[a], kb[fs, :], preferred_element_type=f32)
                     for a, (_, fs) in enumerate(groups)]
            sb = jnp.concatenate([s[0:rg] for s in s_two], axis=0)
            gb = jnp.sum(sb + jnp.concatenate([s[rg:2 * rg] for s in s_two], axis=0),
                         axis=-1, keepdims=True)
            mb = jnp.max(sb, axis=-1, keepdims=True)
            pf = jnp.exp2(sb - mb)
            lb = jnp.sum(pf, axis=-1, keepdims=True)
            stats[bb] = (gb, mb, lb, pf.astype(jnp.bfloat16))

        def values(bb):
            slot = region * pb + bb
            pbf = stats[bb][3]
            vb = jnp.concatenate([vbuf[slot, p].astype(jnp.bfloat16) for p in range(ppb)], axis=1)
            ob = [lax.dot_general(pbf[rs], vb[fs, :], _NT, preferred_element_type=f32)
                  for rs, fs in groups]
            op_ref[n0 + bb] = jnp.concatenate([_fold_heads(o, row_head_g) for o in ob], axis=0)

        skew = 2
        for bb in range(pb + skew):
            if bb < pb:
                scores(bb)
            if bb >= skew:
                values(bb - skew)
        g_new, m_new, l_new = g_ref[...], m_ref[...], l_ref[...]
        for bb in range(pb):
            gb, mb, lb, _ = stats[bb]
            g_new = jnp.where(lane == n0 + bb, gb, g_new)
            m_new = jnp.where(lane == n0 + bb, mb, m_new)
            l_new = jnp.where(lane == n0 + bb, lb, l_new)
        g_ref[...] = g_new
        m_ref[...] = m_new
        l_ref[...] = l_new
        return carry

    lax.fori_loop(0, n_pass, one_pass, 0)

    @pl.when(d == pl.num_programs(0) - 1)
    def _():
        for j in range(RING_PASSES - 1):
            for c in pass_copies(total_pass + j, (total_pass + j) % RING_PASSES):
                c.wait()

    sc = jnp.where(lane < npb, g_ref[...], -jnp.inf)
    sel = jnp.zeros((r, LANES), jnp.bool_)
    for _ in range(min(MOBA_TOPK, npb)):
        best = jnp.max(sc, axis=-1, keepdims=True)
        first = jnp.min(jnp.where(sc == best, lane, LANES), axis=-1, keepdims=True)
        hit = lane == first
        sel = sel | hit
        sc = jnp.where(hit, -jnp.inf, sc)

    kn = kn_ref[...]
    vn = vn_ref[...]
    so = lax.dot_general(q_log, kn.astype(jnp.bfloat16), _NT, preferred_element_type=f32)
    tq = lax.broadcasted_iota(jnp.int32, (r, t), 0) % t
    tk = lax.broadcasted_iota(jnp.int32, (r, t), 1)
    so = jnp.where(tk <= tq, so, NEG)
    m_own = jnp.max(so, axis=-1, keepdims=True)
    p_own = jnp.exp2(so - m_own)
    l_own = jnp.sum(p_own, axis=-1, keepdims=True)
    o_own = _fold_heads(jnp.dot(p_own.astype(jnp.bfloat16), vn.astype(jnp.bfloat16),
                                preferred_element_type=f32), row_head)

    m_sel = jnp.where(sel, m_ref[...], NEG)
    m_tot = jnp.maximum(m_own, jnp.max(m_sel, axis=-1, keepdims=True))
    w = jnp.where(sel, jnp.exp2(m_sel - m_tot), 0.0)
    w_own = jnp.exp2(m_own - m_tot)
    l_tot = w_own * l_own + jnp.sum(w * l_ref[...], axis=-1, keepdims=True)

    def add_block(nn, acc):
        wn = jnp.sum(jnp.where(lane == nn, w, 0.0), axis=-1, keepdims=True)
        return acc + wn * op_ref[nn]

    o_tot = lax.fori_loop(0, npb, add_block, w_own * o_own) / l_tot
    tiles = []
    for p in range(da // LANES):
        lo = o_tot[(2 * p) * t:(2 * p + 1) * t, :]
        hi = o_tot[(2 * p + 1) * t:(2 * p + 2) * t, :]
        tiles.append(jnp.where(lane[0:t] < HEAD_DIM, lo, hi))
    o_ref[...] = jnp.concatenate(tiles, axis=-1).astype(o_ref.dtype)


def _moba_sample(q, k_new, v_new, pool_k, pool_v, page_table, *, t):
    n, da = q.shape
    db, n_pages = page_table.shape
    n_phys, page = pool_k.shape[0], pool_k.shape[1]
    ppb = MOBA_BLOCK // page
    npb = (n_pages * page) // MOBA_BLOCK
    assert MOBA_BLOCK % page == 0 and n_pages == npb * ppb and npb >= 1 and npb <= LANES
    pb = PASS_BLOCKS
    slots = RING_PASSES * pb
    assert npb % pb == 0 and db * (npb // pb) >= RING_PASSES and t % SUBLANES == 0 and page % LANES == 0
    pk = jnp.transpose(pool_k, (0, 2, 3, 1)).reshape(n_phys, da, page)
    pv = jnp.transpose(pool_v, (0, 2, 3, 1)).reshape(n_phys, da, page)
    r = N_HEADS * t
    row_spec = pl.BlockSpec((t, da), lambda d, pt: (d, 0))
    hbm_spec = pl.BlockSpec(memory_space=pl.ANY)
    grid_spec = pltpu.PrefetchScalarGridSpec(
        num_scalar_prefetch=1,
        grid=(db,),
        in_specs=[row_spec, row_spec, row_spec, hbm_spec, hbm_spec],
        out_specs=row_spec,
        scratch_shapes=[pltpu.VMEM((slots, ppb, da, page), jnp.float32),
                        pltpu.VMEM((slots, ppb, da, page), jnp.float32),
                        pltpu.SemaphoreType.DMA((slots, ppb)),
                        pltpu.SemaphoreType.DMA((slots, ppb)),
                        pltpu.VMEM((r, LANES), jnp.float32),
                        pltpu.VMEM((r, LANES), jnp.float32),
                        pltpu.VMEM((r, LANES), jnp.float32),
                        pltpu.VMEM((npb, r, LANES), jnp.float32)],
    )
    return pl.pallas_call(
        functools.partial(_moba_sample_kernel, t=t, npb=npb, ppb=ppb, pb=pb),
        grid_spec=grid_spec,
        out_shape=jax.ShapeDtypeStruct((n, da), jnp.bfloat16),
        compiler_params=pltpu.CompilerParams(
            dimension_semantics=("arbitrary",), vmem_limit_bytes=VMEM_LIMIT),
        name="moba_sample",
    )(page_table, q, k_new, v_new, pk, pv)


def _out_ffn_kernel(x_ref, a_ref, c_ref, hist_ref, wo_ref, gpm_ref, gpf_ref, wup_ref, wdw_ref, bdw_ref,
                    wdn_ref, gpo_ref, y_ref, st_ref, carry_ref, f_ref, *, gt, l, width):
    s = pl.program_id(1)
    d_ff = wdn_ref.shape[0]
    da = a_ref.shape[-1]
    pad = carry_ref.shape[1]
    h0 = pad - (width - 1)

    @pl.when(s == 0)
    def _():
        carry_ref[:, h0:pad, :] = hist_ref[...]

    mix = (jnp.dot(a_ref[...], wo_ref[0:da, :], preferred_element_type=jnp.float32)
           + jnp.dot(c_ref[...], wo_ref[da:, :], preferred_element_type=jnp.float32))
    x1 = x_ref[...] + _rms(mix, gpm_ref[...])
    h = _rms(x1, gpf_ref[...]).astype(jnp.bfloat16)

    def up_proj(lo):
        return jnp.dot(h, wup_ref[:, lo:lo + FF_CHUNK], preferred_element_type=jnp.float32)

    def conv(u, lo):
        hi = lo + FF_CHUNK
        u3 = u.reshape(gt, l, FF_CHUNK)
        ext = jnp.concatenate([carry_ref[:, :, lo:hi], u3], axis=1)
        carry_ref[:, :, lo:hi] = u3[:, l - pad:, :]
        acc = u3 * wdw_ref[width - 1:width, lo:hi]
        for j in range(width - 1):
            back = width - 1 - j
            acc = acc + pltpu.roll(ext, back, 1)[:, pad:, :] * wdw_ref[j:j + 1, lo:hi]
        return (acc + bdw_ref[:, lo:hi]).reshape(gt * l, FF_CHUNK)

    n_cc = d_ff // FF_CHUNK
    ups = {}
    for cc in range(n_cc + 1):
        if cc < n_cc:
            ups[cc] = (up_proj(cc * FF_CHUNK), up_proj(d_ff + cc * FF_CHUNK))
        if cc >= 1:
            ug, uv = ups.pop(cc - 1)
            gate = conv(ug, (cc - 1) * FF_CHUNK)
            val = conv(uv, d_ff + (cc - 1) * FF_CHUNK)
            f_ref[:, (cc - 1) * FF_CHUNK:cc * FF_CHUNK] = (_silu(gate) * val).astype(jnp.bfloat16)

    ff = jnp.dot(f_ref[...], wdn_ref[...], preferred_element_type=jnp.float32)
    y_ref[...] = x1 + _rms(ff, gpo_ref[...])

    @pl.when(s == pl.num_programs(1) - 1)
    def _():
        st_ref[...] = carry_ref[:, h0:pad, :]


def _out_ffn(x, attn, c, hist, wo, gpm, gpf, wup, wdw, bdw, wdn, gpo, *, gt, l):
    g_total, s_total, d = x.shape
    n = g_total * s_total
    d_ff = wdn.shape[0]
    width = wdw.shape[0]
    pad = SUBLANES
    assert width - 1 <= pad <= l and d_ff % FF_CHUNK == 0
    rows = gt * l
    n_s = s_total // l
    assert s_total % l == 0 and g_total % gt == 0 and l % SUBLANES == 0
    assert gt == 1 or n_s == 1
    row_spec = lambda cdim: pl.BlockSpec((rows, cdim), lambda g, s: (g * n_s + s, 0))
    st_spec = pl.BlockSpec((gt, width - 1, 2 * d_ff), lambda g, s: (g, 0, 0))
    f32 = jnp.float32
    return pl.pallas_call(
        functools.partial(_out_ffn_kernel, gt=gt, l=l, width=width),
        grid=(g_total // gt, n_s),
        in_specs=[
            row_spec(d), row_spec(attn.shape[1]), row_spec(c.shape[1]), st_spec,
            _const_spec(wo.shape), _const_spec((1, d)), _const_spec((1, d)),
            _const_spec(wup.shape), _const_spec(wdw.shape), _const_spec((1, 2 * d_ff)),
            _const_spec(wdn.shape), _const_spec((1, d)),
        ],
        out_specs=[row_spec(d), st_spec],
        out_shape=[jax.ShapeDtypeStruct((n, d), f32),
                   jax.ShapeDtypeStruct((g_total, width - 1, 2 * d_ff), f32)],
        scratch_shapes=[pltpu.VMEM((gt, pad, 2 * d_ff), f32),
                        pltpu.VMEM((rows, d_ff), jnp.bfloat16)],
        compiler_params=pltpu.CompilerParams(
            dimension_semantics=("arbitrary", "arbitrary"), vmem_limit_bytes=VMEM_LIMIT),
        name="out_ffn",
    )(x.reshape(n, d), attn, c, hist, wo, gpm.reshape(1, d), gpf.reshape(1, d), wup, wdw,
      bdw.reshape(1, -1), wdn, gpo.reshape(1, d))


def kernel(x_prompt, x_sample, cache_k, cache_v, page_table, state_cv, state_ffn, g_pre_mix, w_in, b_in,
           w_cv_dw, b_cv_dw, g_cv_ln, b_cv_ln, w_cv_pw, b_cv_pw, w_out, g_post_mix, g_pre_ffn, w_up,
           w_ffn_dw, b_ffn_dw, w_down, g_post_ffn):
    depth = w_in.shape[0]
    bf16 = jnp.bfloat16
    b, s_len, d = x_prompt.shape
    db, t, _ = x_sample.shape
    n_qkv = 3 * D_ATTN
    kv_s = (db, t, N_HEADS, HEAD_DIM)
    yp, ys = x_prompt, x_sample
    outs = [[] for _ in range(8)]
    for lyr in range(depth):
        w_qkv = w_in[lyr][:, :n_qkv].astype(bf16)
        b_qkv = b_in[lyr][:n_qkv]
        cv_prm = (w_in[lyr][:, n_qkv:].astype(bf16), b_in[lyr][n_qkv:], w_cv_dw[lyr], b_cv_dw[lyr],
                  g_cv_ln[lyr], b_cv_ln[lyr], w_cv_pw[lyr].astype(bf16), b_cv_pw[lyr])
        ffn_prm = (w_out[lyr].astype(bf16), g_post_mix[lyr], g_pre_ffn[lyr], w_up[lyr].astype(bf16),
                   w_ffn_dw[lyr], b_ffn_dw[lyr], w_down[lyr].astype(bf16), g_post_ffn[lyr])

        cv0 = jnp.zeros((b,) + state_cv.shape[2:], x_prompt.dtype)
        ff0 = jnp.zeros((b,) + state_ffn.shape[2:], x_prompt.dtype)
        lp = min(ROW_TILE, s_len)
        qt, kt, vt, c, cvp = _inproj_conv(yp, cv0, g_pre_mix[lyr], w_qkv.T, b_qkv.reshape(n_qkv, 1), *cv_prm,
                                          gt=1, l=lp, transposed=True)
        attn = _moba_prompt(qt, kt, vt).reshape(b * s_len, D_ATTN)
        yp, ffp = _out_ffn(yp, attn, c, ff0, *ffn_prm, gt=1, l=lp)
        yp = yp.reshape(b, s_len, d)
        kp = jnp.transpose(kt.reshape(b, N_HEADS, HEAD_DIM, s_len), (0, 3, 1, 2))
        vp = jnp.transpose(vt.reshape(b, N_HEADS, HEAD_DIM, s_len), (0, 3, 1, 2))

        q, k, v, c, cvs = _inproj_conv(ys, state_cv[lyr], g_pre_mix[lyr], w_qkv, b_qkv.reshape(1, n_qkv), *cv_prm,
                                       gt=db, l=t, transposed=False)
        attn = _moba_sample(q, k, v, cache_k[lyr], cache_v[lyr], page_table, t=t)
        ys, ffs = _out_ffn(ys, attn, c, state_ffn[lyr], *ffn_prm, gt=db, l=t)
        ys = ys.reshape(db, t, d)
        for lst, val in zip(outs, (kp, vp, cvp, ffp, k.reshape(kv_s), v.reshape(kv_s), cvs, ffs)):
            lst.append(val)
    return (yp, ys) + tuple(jnp.stack(lst) for lst in outs)
```

```python
import functools

import jax
import jax.numpy as jnp
from jax import lax
from jax.experimental import pallas as pl
from jax.experimental.pallas import tpu as pltpu

N_HEADS = 8
HEAD_DIM = 64
D_ATTN = N_HEADS * HEAD_DIM
MOBA_BLOCK = 256
MOBA_TOPK = 3
RMS_EPS = 1e-6
LN_EPS = 1e-5
NEG = -1e30
LOG2E = 1.4426950408889634

LANES = 128
SUBLANES = 8
MXU_DIM = 256
VMEM_LIMIT = 56 * 1024 * 1024

ROW_TILE = 512
CONV_CHUNK = 32
FF_CHUNK = 256
PASS_BLOCKS = 8
RING_PASSES = 3

_NT = (((1,), (1,)), ((), ()))
_HI = lax.Precision.HIGHEST


def _const_spec(shape):
    nd = len(shape)
    return pl.BlockSpec(shape, lambda *_: (0,) * nd, pipeline_mode=pl.Buffered(1))


def _rms(x, g):
    return x * lax.rsqrt(jnp.mean(x * x, axis=-1, keepdims=True) + RMS_EPS) * g


def _silu(x):
    return x / (1.0 + jnp.exp(-x))


def _inproj_conv_kernel(*refs, gt, l, width, transposed, stream):
    if stream is not None:
        pt_ref, refs = refs[0], refs[1:]
    (x_ref, hist_ref, gpre_ref, wqkv_ref, bqkv_ref, wcv_ref, bcv_ref, wdw_ref, bdw_ref,
     gln_ref, bln_ref, wpw_ref, bpw_ref) = refs[:13]
    refs = refs[13:]
    if stream is not None:
        stream_in, refs = refs[:5], refs[5:]
    q_ref, k_ref, v_ref, c_ref, cvs_ref = refs[:5]
    refs = refs[5:]
    if stream is not None:
        stream_out, refs = refs[0], refs[1:]
    ext_ref, act_ref = refs[:2]
    s = pl.program_id(1)
    if stream is not None:
        n_steps = pl.num_programs(0) * pl.num_programs(1)
        step = pl.program_id(0) * pl.num_programs(1) + s
        s_head, s_passes, s_tail = _sample_stream(step, n_steps, stream["pps"], pt_ref, *stream_in, stream_out,
                                                  *refs[2:], **stream["dims"])
        s_head()
    else:
        s_passes, s_tail = [], None
    dc = act_ref.shape[-1]
    pad = ext_ref.shape[1] - l
    h0 = pad - (width - 1)

    @pl.when(s == 0)
    def _():
        ext_ref[:, h0:pad, :] = hist_ref[...]

    h = _rms(x_ref[...], gpre_ref[...]).astype(jnp.bfloat16)

    cva = jnp.dot(h, wcv_ref[:, 0:dc], preferred_element_type=jnp.float32) + bcv_ref[:, 0:dc]
    cvg = jnp.dot(h, wcv_ref[:, dc:2 * dc], preferred_element_type=jnp.float32) + bcv_ref[:, dc:2 * dc]
    glu = cva / (1.0 + jnp.exp(-cvg))
    ext_ref[:, pad:pad + l, :] = glu.reshape(gt, l, dc)

    def qkv_proj(idx, ref):
        lo, hi = idx * D_ATTN, (idx + 1) * D_ATTN
        if transposed:
            ref[0] = lax.dot_general(wqkv_ref[lo:hi, :], h, _NT,
                                     preferred_element_type=jnp.float32) + bqkv_ref[lo:hi, :]
        else:
            ref[...] = jnp.dot(h, wqkv_ref[:, lo:hi], preferred_element_type=jnp.float32) + bqkv_ref[:, lo:hi]

    rc = min(l, CONV_CHUNK)
    n_chunks = l // rc
    chunks = [(g, cidx * rc) for g in range(gt) for cidx in range(n_chunks)]
    pending = [(len(chunks) * (idx + 1) // 4, functools.partial(qkv_proj, idx, ref))
               for idx, ref in enumerate((q_ref, k_ref, v_ref))]
    pending += [(len(chunks) * kk // len(s_passes), one_pass) for kk, one_pass in enumerate(s_passes)]
    pending.sort(key=lambda item: item[0])
    for it in range(len(chunks) + 1):
        while pending and (it == len(chunks) or pending[0][0] <= it):
            pending.pop(0)[1]()
        if it == len(chunks):
            break
        g, r0 = chunks[it]
        for lt in range(dc // LANES):
            ls = slice(lt * LANES, (lt + 1) * LANES)
            win = ext_ref[g, r0:r0 + rc + pad, ls]
            shifted = {r: pltpu.roll(win, rc + pad - r, 0) for r in range(1, SUBLANES)}
            acc = None
            for j in range(width):
                a, r = divmod(h0 + j, SUBLANES)
                lo = a * SUBLANES
                tap = ext_ref[g, r0 + lo:r0 + lo + rc, ls] if r == 0 else shifted[r][lo:lo + rc]
                term = tap * wdw_ref[j:j + 1, ls]
                acc = term if acc is None else acc + term
            act_ref[g * l + r0:g * l + r0 + rc, ls] = acc + bdw_ref[:, ls]
        cv = act_ref[g * l + r0:g * l + r0 + rc, :]
        mu = jnp.mean(cv, axis=-1, keepdims=True)
        cen = cv - mu
        var = jnp.mean(cen * cen, axis=-1, keepdims=True)
        y = cen * lax.rsqrt(var + LN_EPS) * gln_ref[...] + bln_ref[...]
        act_ref[g * l + r0:g * l + r0 + rc, :] = _silu(y)

    c = jnp.dot(act_ref[...].astype(jnp.bfloat16), wpw_ref[...],
                preferred_element_type=jnp.float32) + bpw_ref[...]
    c_ref[...] = c.astype(c_ref.dtype)

    @pl.when(s == pl.num_programs(1) - 1)
    def _():
        cvs_ref[...] = ext_ref[:, l + h0:l + pad, :]

    ext_ref[:, 0:pad, :] = ext_ref[:, l:l + pad, :]

    if s_tail is not None:
        s_tail()


def _inproj_conv(x, hist, gpre, wqkv, bqkv, wcv, bcv, wdw, bdw, gln, bln, wpw, bpw, *, gt, l, transposed,
                 sample=None):
    g_total, s_total, d = x.shape
    n = g_total * s_total
    dc = wpw.shape[0]
    width = wdw.shape[0]
    pad = -(-(width - 1) // SUBLANES) * SUBLANES
    rows = gt * l
    n_s = s_total // l
    assert s_total % l == 0 and g_total % gt == 0 and l % SUBLANES == 0
    assert gt == 1 or n_s == 1
    grid = (g_total // gt, n_s)
    row_spec = lambda c: pl.BlockSpec((rows, c), lambda g, s, *_: (g * n_s + s, 0))
    hist_spec = pl.BlockSpec((gt, width - 1, dc), lambda g, s, *_: (g, 0, 0))
    f32 = jnp.float32
    if transposed:
        assert gt == 1
        qkv_spec = pl.BlockSpec((1, D_ATTN, l), lambda g, s, *_: (g, 0, s))
        qkv_shape = jax.ShapeDtypeStruct((g_total, D_ATTN, s_total), f32)
    else:
        qkv_spec = row_spec(D_ATTN)
        qkv_shape = jax.ShapeDtypeStruct((n, D_ATTN), f32)
    in_specs = [
        row_spec(d), hist_spec,
        _const_spec((1, d)), _const_spec(wqkv.shape), _const_spec(bqkv.shape),
        _const_spec(wcv.shape), _const_spec((1, 2 * dc)),
        _const_spec(wdw.shape), _const_spec((1, dc)), _const_spec((1, dc)), _const_spec((1, dc)),
        _const_spec(wpw.shape), _const_spec((1, dc)),
    ]
    out_specs = [qkv_spec, qkv_spec, qkv_spec, row_spec(dc), hist_spec]
    out_shape = [qkv_shape, qkv_shape, qkv_shape, jax.ShapeDtypeStruct((n, dc), jnp.bfloat16),
                 jax.ShapeDtypeStruct((g_total, width - 1, dc), f32)]
    scratch = [pltpu.VMEM((gt, pad + l, dc), f32), pltpu.VMEM((rows, dc), f32)]
    args = [x.reshape(n, d), hist, gpre.reshape(1, d), wqkv, bqkv, wcv, bcv.reshape(1, 2 * dc), wdw,
            bdw.reshape(1, dc), gln.reshape(1, dc), bln.reshape(1, dc), wpw, bpw.reshape(1, dc)]
    stream = None
    prefetch = []
    if sample is not None:
        qs, ks, vs, pool_k, pool_v, page_table, t = sample
        da = qs.shape[1]
        db, n_pages = page_table.shape
        n_phys, page = pool_k.shape[0], pool_k.shape[1]
        ppb = MOBA_BLOCK // page
        npb = (n_pages * page) // MOBA_BLOCK
        assert MOBA_BLOCK % page == 0 and n_pages == npb * ppb and npb >= 1 and npb <= LANES
        pb = PASS_BLOCKS
        slots = RING_PASSES * pb
        n_steps = grid[0] * grid[1]
        assert npb % pb == 0 and t % SUBLANES == 0 and page % LANES == 0
        total_pass = db * (npb // pb)
        assert total_pass % n_steps == 0 and total_pass >= RING_PASSES
        pps = total_pass // n_steps
        assert (npb // pb) % pps == 0
        spd = (npb // pb) // pps
        pk = jnp.transpose(pool_k, (0, 2, 3, 1)).reshape(n_phys, da, page)
        pv = jnp.transpose(pool_v, (0, 2, 3, 1)).reshape(n_phys, da, page)
        r = N_HEADS * t
        seq_spec = pl.BlockSpec((t, da), lambda g, s, *_: ((g * n_s + s) // spd, 0))
        hbm_spec = pl.BlockSpec(memory_space=pl.ANY)
        in_specs += [seq_spec, seq_spec, seq_spec, hbm_spec, hbm_spec]
        out_specs.append(seq_spec)
        out_shape.append(jax.ShapeDtypeStruct(qs.shape, jnp.bfloat16))
        scratch += [pltpu.VMEM((slots, ppb, da, page), f32),
                    pltpu.VMEM((slots, ppb, da, page), f32),
                    pltpu.SemaphoreType.DMA((slots, ppb)),
                    pltpu.SemaphoreType.DMA((slots, ppb)),
                    pltpu.VMEM((r, LANES), f32),
                    pltpu.VMEM((r, LANES), f32),
                    pltpu.VMEM((r, LANES), f32),
                    pltpu.VMEM((npb, r, LANES), f32)]
        args += [qs, ks, vs, pk, pv]
        prefetch = [page_table]
        stream = dict(pps=pps, dims=dict(t=t, npb=npb, ppb=ppb, pb=pb))
    grid_spec = pltpu.PrefetchScalarGridSpec(
        num_scalar_prefetch=len(prefetch), grid=grid, in_specs=in_specs, out_specs=out_specs,
        scratch_shapes=scratch)
    return pl.pallas_call(
        functools.partial(_inproj_conv_kernel, gt=gt, l=l, width=width, transposed=transposed, stream=stream),
        grid_spec=grid_spec,
        out_shape=out_shape,
        compiler_params=pltpu.CompilerParams(
            dimension_semantics=("arbitrary", "arbitrary"), vmem_limit_bytes=VMEM_LIMIT),
        name="inproj_conv",
    )(*prefetch, *args)


def _moba_prompt_kernel(qt_ref, kt_ref, vt_ref, o_ref, kaug_ref, vaug_ref, s_ref, p_ref, *, nb):
    blk = MOBA_BLOCK
    s_len = kt_ref.shape[2]
    qscale = HEAD_DIM ** -0.5 * LOG2E
    f32 = jnp.float32
    n_buf = s_ref.shape[0]

    kt = kt_ref[0]
    vt = vt_ref[0]
    key_blk = lax.broadcasted_iota(jnp.int32, (nb, s_len), 1) // blk
    ind = jnp.where(key_blk == lax.broadcasted_iota(jnp.int32, (nb, s_len), 0), 1.0 / blk, 0.0)
    km = lax.dot_general(ind, kt, _NT, precision=_HI, preferred_element_type=f32)
    km_lane = lax.broadcasted_iota(jnp.int32, (nb, LANES), 1)

    k_rows = kt.T
    row_blk = lax.broadcasted_iota(jnp.int32, (s_len, LANES), 0) // blk
    lane_s = lax.broadcasted_iota(jnp.int32, (s_len, LANES), 1)
    sub_s = lax.broadcasted_iota(jnp.int32, (LANES, s_len), 0)
    for hh in range(2):
        own_lo = hh * HEAD_DIM
        base = HEAD_DIM - own_lo
        own_s = (lane_s >= own_lo) & (lane_s < own_lo + HEAD_DIM)
        onehot = (lane_s - base == row_blk).astype(f32)
        kaug_ref[hh] = jnp.where(own_s, k_rows, onehot).astype(jnp.bfloat16)
        own_r = (sub_s >= own_lo) & (sub_s < own_lo + HEAD_DIM)
        vaug_ref[hh] = jnp.where(own_r, vt, jnp.where(sub_s == base, 1.0, 0.0)).astype(jnp.bfloat16)

    sub = lax.broadcasted_iota(jnp.int32, (nb, blk), 0)
    rr = lax.broadcasted_iota(jnp.int32, (blk, blk), 0)
    cc = lax.broadcasted_iota(jnp.int32, (blk, blk), 1)
    causal = rr <= cc
    zeros_fill = jnp.zeros((HEAD_DIM - nb, blk), f32)
    items = [(i, hh) for i in range(nb) for hh in range(2)]
    km_both = jnp.concatenate([jnp.where((km_lane >= hh * HEAD_DIM) & (km_lane < (hh + 1) * HEAD_DIM), km, 0.0)
                               for hh in range(2)], axis=0)
    gates = {}
    o_heads = {}

    def logits(n):
        i, hh = items[n]
        w = (i + 1) * blk
        slot = n % n_buf
        own_lo = hh * HEAD_DIM
        qt_i = qt_ref[0, :, i * blk:(i + 1) * blk]
        if hh == 0:
            gates[i] = jnp.dot(km_both, qt_i, precision=_HI, preferred_element_type=f32)
        gate = gates[i][hh * nb:(hh + 1) * nb]
        cnt = jnp.zeros((nb, blk), jnp.int32)
        for m in range(i):
            row = gate[m:m + 1, :]
            beats = (row > gate) | ((row == gate) & (m < sub))
            cnt = cnt + beats.astype(jnp.int32)
        keep = ((sub < i) & (cnt < MOBA_TOPK)) | (sub == i)
        bias = jnp.where(keep, 0.0, NEG)
        q_own = qt_i[own_lo:own_lo + HEAD_DIM, :] * qscale
        parts = [q_own, bias, zeros_fill] if hh == 0 else [bias, zeros_fill, q_own]
        q_aug = jnp.concatenate(parts, axis=0).astype(jnp.bfloat16)
        if i > 0:
            s_ref[slot, 0:w - blk, :] = jnp.dot(kaug_ref[hh, 0:w - blk, :], q_aug, preferred_element_type=f32)
        s_own = jnp.dot(kaug_ref[hh, w - blk:w, :], q_aug, preferred_element_type=f32)
        s_ref[slot, w - blk:w, :] = jnp.where(causal, s_own, NEG)

    def softmax(n):
        i, _ = items[n]
        w = (i + 1) * blk
        slot = n % n_buf
        sv = s_ref[slot, 0:w, :]
        mx = jnp.max(sv, axis=0, keepdims=True)
        p_ref[slot, 0:w, :] = jnp.exp2(sv - mx).astype(jnp.bfloat16)

    def values(n):
        i, hh = items[n]
        w = (i + 1) * blk
        slot = n % n_buf
        own_lo = hh * HEAD_DIM
        base = HEAD_DIM - own_lo
        o_t = jnp.dot(vaug_ref[hh, :, 0:w], p_ref[slot, 0:w, :], preferred_element_type=f32)
        o_heads[hh] = o_t[own_lo:own_lo + HEAD_DIM, :] * (1.0 / o_t[base:base + 1, :])
        if hh == 1:
            o_pair = jnp.concatenate([o_heads[0], o_heads[1]], axis=0)
            o_ref[0, i * blk:(i + 1) * blk, :] = o_pair.T.astype(o_ref.dtype)

    lag = (n_buf - 1) // 2
    for n in range(len(items) + 2 * lag):
        if n < len(items):
            logits(n)
        if 0 <= n - lag < len(items):
            softmax(n - lag)
        if 0 <= n - 2 * lag < len(items):
            values(n - 2 * lag)


def _moba_prompt(qt, kt, vt):
    b, da, s_len = qt.shape
    assert s_len % MOBA_BLOCK == 0 and da % LANES == 0 and LANES == 2 * HEAD_DIM
    nb = s_len // MOBA_BLOCK
    assert nb <= HEAD_DIM
    n_buf = 5
    spec = pl.BlockSpec((1, LANES, s_len), lambda bi, hp: (bi, hp, 0))
    return pl.pallas_call(
        functools.partial(_moba_prompt_kernel, nb=nb),
        grid=(b, da // LANES),
        in_specs=[spec, spec, spec],
        out_specs=pl.BlockSpec((1, s_len, LANES), lambda bi, hp: (bi, 0, hp)),
        out_shape=jax.ShapeDtypeStruct((b, s_len, da), jnp.bfloat16),
        scratch_shapes=[pltpu.VMEM((2, s_len, LANES), jnp.bfloat16),
                        pltpu.VMEM((2, LANES, s_len), jnp.bfloat16),
                        pltpu.VMEM((n_buf, s_len, MOBA_BLOCK), jnp.float32),
                        pltpu.VMEM((n_buf, s_len, MOBA_BLOCK), jnp.bfloat16)],
        compiler_params=pltpu.CompilerParams(
            dimension_semantics=("arbitrary", "arbitrary"), vmem_limit_bytes=VMEM_LIMIT),
        name="moba_prompt",
    )(qt, kt, vt)


def _fold_heads(o, row_head):
    r, da = o.shape
    lane_head = lax.broadcasted_iota(jnp.int32, (r, da), 1) // HEAD_DIM
    om = jnp.where(lane_head == row_head, o, 0.0)
    out = om[:, 0:LANES]
    for p in range(1, da // LANES):
        out = out + om[:, p * LANES:(p + 1) * LANES]
    return out


def _sample_stream(step, n_steps, pps, pt_ref, q_ref, kn_ref, vn_ref, pk_ref, pv_ref, o_ref,
                   kbuf, vbuf, ksem, vsem, g_ref, m_ref, l_ref, op_ref, *, t, npb, ppb, pb):
    r = N_HEADS * t
    da = q_ref.shape[-1]
    n_pass = npb // pb
    spd = n_pass // pps
    total_pass = n_steps * pps
    scale = HEAD_DIM ** -0.5 * LOG2E
    f32 = jnp.float32

    def pass_copies(gp, region):
        gp = jnp.minimum(gp, total_pass - 1)
        copies = []
        for bb in range(pb):
            g = gp * pb + bb
            slot = region * pb + bb
            for p in range(ppb):
                page = pt_ref[g // npb, (g % npb) * ppb + p]
                copies.append(pltpu.make_async_copy(pk_ref.at[page], kbuf.at[slot, p], ksem.at[slot, p]))
                copies.append(pltpu.make_async_copy(pv_ref.at[page], vbuf.at[slot, p], vsem.at[slot, p]))
        return copies

    def start_all(copies):
        for idx, c in enumerate(copies):
            c.start(priority=idx % 2)

    def head():
        @pl.when(step == 0)
        def _():
            for gp in range(RING_PASSES - 1):
                start_all(pass_copies(gp, gp))

        @pl.when(step % spd == 0)
        def _():
            g_ref[...] = jnp.zeros(g_ref.shape, g_ref.dtype)
            m_ref[...] = jnp.full(m_ref.shape, NEG, m_ref.dtype)
            l_ref[...] = jnp.zeros(l_ref.shape, l_ref.dtype)

    q = q_ref[...]
    q_rep = jnp.concatenate([q] * N_HEADS, axis=0)
    row_head = lax.broadcasted_iota(jnp.int32, (r, da), 0) // t
    lane_head = lax.broadcasted_iota(jnp.int32, (r, da), 1) // HEAD_DIM
    q_exp = jnp.where(row_head == lane_head, q_rep, 0.0)
    q_s = q_exp * scale
    q_log = q_s.astype(jnp.bfloat16)
    q_lo = (q_s - q_log.astype(f32)).astype(jnp.bfloat16)
    rg = (MXU_DIM // HEAD_DIM) * t
    groups = [(slice(a * rg, (a + 1) * rg), slice(a * MXU_DIM, (a + 1) * MXU_DIM)) for a in range(da // MXU_DIM)]
    q_two = [jnp.concatenate([q_log[rs, fs], q_lo[rs, fs]], axis=0) for rs, fs in groups]
    row_head_g = lax.broadcasted_iota(jnp.int32, (rg, MXU_DIM), 0) // t
    lane = lax.broadcasted_iota(jnp.int32, (r, LANES), 1)

    def one_pass(kk):
        gp = step * pps + kk
        n0 = (gp % n_pass) * pb
        region = gp % RING_PASSES
        for c in pass_copies(gp, region):
            c.wait()
        start_all(pass_copies(gp + RING_PASSES - 1, (gp + RING_PASSES - 1) % RING_PASSES))
        stats = {}

        def scores(bb):
            slot = region * pb + bb
            kb = jnp.concatenate([kbuf[slot, p].astype(jnp.bfloat16) for p in range(ppb)], axis=1)
            s_two = [jnp.dot(q_two[a], kb[fs, :], preferred_element_type=f32)
                     for a, (_, fs) in enumerate(groups)]
            sb = jnp.concatenate([s[0:rg] for s in s_two], axis=0)
            gb = jnp.sum(sb + jnp.concatenate([s[rg:2 * rg] for s in s_two], axis=0),
                         axis=-1, keepdims=True)
            mb = jnp.max(sb, axis=-1, keepdims=True)
            pf = jnp.exp2(sb - mb)
            lb = jnp.sum(pf, axis=-1, keepdims=True)
            stats[bb] = (gb, mb, lb, pf.astype(jnp.bfloat16))

        def values(bb):
            slot = region * pb + bb
            pbf = stats[bb][3]
            vb = jnp.concatenate([vbuf[slot, p].astype(jnp.bfloat16) for p in range(ppb)], axis=1)
            ob = [lax.dot_general(pbf[rs], vb[fs, :], _NT, preferred_element_type=f32)
                  for rs, fs in groups]
            op_ref[n0 + bb] = jnp.concatenate([_fold_heads(o, row_head_g) for o in ob], axis=0)

        skew = 2
        for bb in range(pb + skew):
            if bb < pb:
                scores(bb)
            if bb >= skew:
                values(bb - skew)
        g_new, m_new, l_new = g_ref[...], m_ref[...], l_ref[...]
        for bb in range(pb):
            gb, mb, lb, _ = stats[bb]
            g_new = jnp.where(lane == n0 + bb, gb, g_new)
            m_new = jnp.where(lane == n0 + bb, mb, m_new)
            l_new = jnp.where(lane == n0 + bb, lb, l_new)
        g_ref[...] = g_new
        m_ref[...] = m_new
        l_ref[...] = l_new

    def tail():
        @pl.when(step == n_steps - 1)
        def _():
            for j in range(RING_PASSES - 1):
                for c in pass_copies(total_pass + j, (total_pass + j) % RING_PASSES):
                    c.wait()

        @pl.when(step % spd == spd - 1)
        def _():
            sc = jnp.where(lane < npb, g_ref[...], -jnp.inf)
            sel = jnp.zeros((r, LANES), jnp.bool_)
            for _ in range(min(MOBA_TOPK, npb)):
                best = jnp.max(sc, axis=-1, keepdims=True)
                first = jnp.min(jnp.where(sc == best, lane, LANES), axis=-1, keepdims=True)
                hit = lane == first
                sel = sel | hit
                sc = jnp.where(hit, -jnp.inf, sc)

            kn = kn_ref[...]
            vn = vn_ref[...]
            so = lax.dot_general(q_log, kn.astype(jnp.bfloat16), _NT, preferred_element_type=f32)
            tq = lax.broadcasted_iota(jnp.int32, (r, t), 0) % t
            tk = lax.broadcasted_iota(jnp.int32, (r, t), 1)
            so = jnp.where(tk <= tq, so, NEG)
            m_own = jnp.max(so, axis=-1, keepdims=True)
            p_own = jnp.exp2(so - m_own)
            l_own = jnp.sum(p_own, axis=-1, keepdims=True)
            o_own = _fold_heads(jnp.dot(p_own.astype(jnp.bfloat16), vn.astype(jnp.bfloat16),
                                        preferred_element_type=f32), row_head)

            m_sel = jnp.where(sel, m_ref[...], NEG)
            m_tot = jnp.maximum(m_own, jnp.max(m_sel, axis=-1, keepdims=True))
            w = jnp.where(sel, jnp.exp2(m_sel - m_tot), 0.0)
            w_own = jnp.exp2(m_own - m_tot)
            l_tot = w_own * l_own + jnp.sum(w * l_ref[...], axis=-1, keepdims=True)

            def add_block(nn, acc):
                wn = jnp.sum(jnp.where(lane == nn, w, 0.0), axis=-1, keepdims=True)
                return acc + wn * op_ref[nn]

            o_tot = lax.fori_loop(0, npb, add_block, w_own * o_own) / l_tot
            tiles = []
            for p in range(da // LANES):
                lo = o_tot[(2 * p) * t:(2 * p + 1) * t, :]
                hi = o_tot[(2 * p + 1) * t:(2 * p + 2) * t, :]
                tiles.append(jnp.where(lane[0:t] < HEAD_DIM, lo, hi))
            o_ref[...] = jnp.concatenate(tiles, axis=-1).astype(o_ref.dtype)

    return head, [functools.partial(one_pass, kk) for kk in range(pps)], tail


def _out_ffn_kernel(x_ref, a_ref, c_ref, hist_ref, wo_ref, gpm_ref, gpf_ref, wup_ref, wdw_ref, bdw_ref,
                    wdn_ref, gpo_ref, y_ref, st_ref, carry_ref, f_ref, *, gt, l, width):
    s = pl.program_id(1)
    d_ff = wdn_ref.shape[0]
    da = a_ref.shape[-1]
    pad = carry_ref.shape[1]
    h0 = pad - (width - 1)

    @pl.when(s == 0)
    def _():
        carry_ref[:, h0:pad, :] = hist_ref[...]

    mix = (jnp.dot(a_ref[...], wo_ref[0:da, :], preferred_element_type=jnp.float32)
           + jnp.dot(c_ref[...], wo_ref[da:, :], preferred_element_type=jnp.float32))
    x1 = x_ref[...] + _rms(mix, gpm_ref[...])
    h = _rms(x1, gpf_ref[...]).astype(jnp.bfloat16)

    def up_proj(lo):
        return jnp.dot(h, wup_ref[:, lo:lo + FF_CHUNK], preferred_element_type=jnp.float32)

    def conv(u, lo):
        hi = lo + FF_CHUNK
        u3 = u.reshape(gt, l, FF_CHUNK)
        ext = jnp.concatenate([carry_ref[:, :, lo:hi], u3], axis=1)
        carry_ref[:, :, lo:hi] = u3[:, l - pad:, :]
        acc = u3 * wdw_ref[width - 1:width, lo:hi]
        for j in range(width - 1):
            back = width - 1 - j
            acc = acc + pltpu.roll(ext, back, 1)[:, pad:, :] * wdw_ref[j:j + 1, lo:hi]
        return (acc + bdw_ref[:, lo:hi]).reshape(gt * l, FF_CHUNK)

    n_cc = d_ff // FF_CHUNK
    ups = {}
    for cc in range(n_cc + 1):
        if cc < n_cc:
            ups[cc] = (up_proj(cc * FF_CHUNK), up_proj(d_ff + cc * FF_CHUNK))
        if cc >= 1:
            ug, uv = ups.pop(cc - 1)
            gate = conv(ug, (cc - 1) * FF_CHUNK)
            val = conv(uv, d_ff + (cc - 1) * FF_CHUNK)
            f_ref[:, (cc - 1) * FF_CHUNK:cc * FF_CHUNK] = (_silu(gate) * val).astype(jnp.bfloat16)

    ff = jnp.dot(f_ref[...], wdn_ref[...], preferred_element_type=jnp.float32)
    y_ref[...] = x1 + _rms(ff, gpo_ref[...])

    @pl.when(s == pl.num_programs(1) - 1)
    def _():
        st_ref[...] = carry_ref[:, h0:pad, :]


def _out_ffn(x, attn, c, hist, wo, gpm, gpf, wup, wdw, bdw, wdn, gpo, *, gt, l):
    g_total, s_total, d = x.shape
    n = g_total * s_total
    d_ff = wdn.shape[0]
    width = wdw.shape[0]
    pad = SUBLANES
    assert width - 1 <= pad <= l and d_ff % FF_CHUNK == 0
    rows = gt * l
    n_s = s_total // l
    assert s_total % l == 0 and g_total % gt == 0 and l % SUBLANES == 0
    assert gt == 1 or n_s == 1
    row_spec = lambda cdim: pl.BlockSpec((rows, cdim), lambda g, s: (g * n_s + s, 0))
    st_spec = pl.BlockSpec((gt, width - 1, 2 * d_ff), lambda g, s: (g, 0, 0))
    f32 = jnp.float32
    return pl.pallas_call(
        functools.partial(_out_ffn_kernel, gt=gt, l=l, width=width),
        grid=(g_total // gt, n_s),
        in_specs=[
            row_spec(d), row_spec(attn.shape[1]), row_spec(c.shape[1]), st_spec,
            _const_spec(wo.shape), _const_spec((1, d)), _const_spec((1, d)),
            _const_spec(wup.shape), _const_spec(wdw.shape), _const_spec((1, 2 * d_ff)),
            _const_spec(wdn.shape), _const_spec((1, d)),
        ],
        out_specs=[row_spec(d), st_spec],
        out_shape=[jax.ShapeDtypeStruct((n, d), f32),
                   jax.ShapeDtypeStruct((g_total, width - 1, 2 * d_ff), f32)],
        scratch_shapes=[pltpu.VMEM((gt, pad, 2 * d_ff), f32),
                        pltpu.VMEM((rows, d_ff), jnp.bfloat16)],
        compiler_params=pltpu.CompilerParams(
            dimension_semantics=("arbitrary", "arbitrary"), vmem_limit_bytes=VMEM_LIMIT),
        name="out_ffn",
    )(x.reshape(n, d), attn, c, hist, wo, gpm.reshape(1, d), gpf.reshape(1, d), wup, wdw,
      bdw.reshape(1, -1), wdn, gpo.reshape(1, d))


def kernel(x_prompt, x_sample, cache_k, cache_v, page_table, state_cv, state_ffn, g_pre_mix, w_in, b_in,
           w_cv_dw, b_cv_dw, g_cv_ln, b_cv_ln, w_cv_pw, b_cv_pw, w_out, g_post_mix, g_pre_ffn, w_up,
           w_ffn_dw, b_ffn_dw, w_down, g_post_ffn):
    depth = w_in.shape[0]
    bf16 = jnp.bfloat16
    b, s_len, d = x_prompt.shape
    db, t, _ = x_sample.shape
    n_qkv = 3 * D_ATTN
    kv_s = (db, t, N_HEADS, HEAD_DIM)
    yp, ys = x_prompt, x_sample
    outs = [[] for _ in range(8)]
    for lyr in range(depth):
        w_qkv = w_in[lyr][:, :n_qkv].astype(bf16)
        b_qkv = b_in[lyr][:n_qkv]
        cv_prm = (w_in[lyr][:, n_qkv:].astype(bf16), b_in[lyr][n_qkv:], w_cv_dw[lyr], b_cv_dw[lyr],
                  g_cv_ln[lyr], b_cv_ln[lyr], w_cv_pw[lyr].astype(bf16), b_cv_pw[lyr])
        ffn_prm = (w_out[lyr].astype(bf16), g_post_mix[lyr], g_pre_ffn[lyr], w_up[lyr].astype(bf16),
                   w_ffn_dw[lyr], b_ffn_dw[lyr], w_down[lyr].astype(bf16), g_post_ffn[lyr])

        q, k, v, c_s, cvs = _inproj_conv(ys, state_cv[lyr], g_pre_mix[lyr], w_qkv, b_qkv.reshape(1, n_qkv), *cv_prm,
                                         gt=db, l=t, transposed=False)

        cv0 = jnp.zeros((b,) + state_cv.shape[2:], x_prompt.dtype)
        ff0 = jnp.zeros((b,) + state_ffn.shape[2:], x_prompt.dtype)
        lp = min(ROW_TILE, s_len)
        qt, kt, vt, c, cvp, attn_s = _inproj_conv(
            yp, cv0, g_pre_mix[lyr], w_qkv.T, b_qkv.reshape(n_qkv, 1), *cv_prm, gt=1, l=lp, transposed=True,
            sample=(q, k, v, cache_k[lyr], cache_v[lyr], page_table, t))
        attn = _moba_prompt(qt, kt, vt).reshape(b * s_len, D_ATTN)
        yp, ffp = _out_ffn(yp, attn, c, ff0, *ffn_prm, gt=1, l=lp)
        yp = yp.reshape(b, s_len, d)
        kp = jnp.transpose(kt.reshape(b, N_HEADS, HEAD_DIM, s_len), (0, 3, 1, 2))
        vp = jnp.transpose(vt.reshape(b, N_HEADS, HEAD_DIM, s_len), (0, 3, 1, 2))

        ys, ffs = _out_ffn(ys, attn_s, c_s, state_ffn[lyr], *ffn_prm, gt=db, l=t)
        ys = ys.reshape(db, t, d)
        for lst, val in zip(outs, (kp, vp, cvp, ffp, k.reshape(kv_s), v.reshape(kv_s), cvs, ffs)):
            lst.append(val)
    return (yp, ys) + tuple(jnp.stack(lst) for lst in outs)
```

```python
import functools

import jax
import jax.numpy as jnp
from jax import lax
from jax.experimental import pallas as pl
from jax.experimental.pallas import tpu as pltpu

N_HEADS = 8
HEAD_DIM = 64
D_ATTN = N_HEADS * HEAD_DIM
MOBA_BLOCK = 256
MOBA_TOPK = 3
RMS_EPS = 1e-6
LN_EPS = 1e-5
NEG = -1e30
LOG2E = 1.4426950408889634

LANES = 128
SUBLANES = 8
MXU_DIM = 256
VMEM_LIMIT = 56 * 1024 * 1024

ROW_TILE = 512
CONV_CHUNK = 32
FF_CHUNK = 256
PASS_BLOCKS = 8
RING_PASSES = 3

_NT = (((1,), (1,)), ((), ()))
_HI = lax.Precision.HIGHEST


def _const_spec(shape):
    nd = len(shape)
    return pl.BlockSpec(shape, lambda *_: (0,) * nd, pipeline_mode=pl.Buffered(1))


def _rms(x, g):
    return x * lax.rsqrt(jnp.mean(x * x, axis=-1, keepdims=True) + RMS_EPS) * g


def _silu(x):
    return x / (1.0 + jnp.exp(-x))


def _inproj_conv_kernel(*refs, gt, l, width, transposed, stream):
    if stream is not None:
        pt_ref, refs = refs[0], refs[1:]
    (x_ref, hist_ref, gpre_ref, wqkv_ref, bqkv_ref, wcv_ref, bcv_ref, wdw_ref, bdw_ref,
     gln_ref, bln_ref, wpw_ref, bpw_ref) = refs[:13]
    refs = refs[13:]
    if stream is not None:
        stream_in, refs = refs[:5], refs[5:]
    q_ref, k_ref, v_ref, c_ref, cvs_ref = refs[:5]
    refs = refs[5:]
    if stream is not None:
        stream_out, refs = refs[0], refs[1:]
    ext_ref, act_ref = refs[:2]
    s = pl.program_id(1)
    if stream is not None:
        n_steps = pl.num_programs(0) * pl.num_programs(1)
        step = pl.program_id(0) * pl.num_programs(1) + s
        s_head, s_passes, s_tail = _sample_stream(step, n_steps, stream["pps"], pt_ref, *stream_in, stream_out,
                                                  *refs[2:], **stream["dims"])
        s_head()
    else:
        s_passes, s_tail = [], None
    dc = act_ref.shape[-1]
    pad = ext_ref.shape[1] - l
    h0 = pad - (width - 1)

    @pl.when(s == 0)
    def _():
        ext_ref[:, h0:pad, :] = hist_ref[...]

    h = _rms(x_ref[...], gpre_ref[...]).astype(jnp.bfloat16)

    cva = jnp.dot(h, wcv_ref[:, 0:dc], preferred_element_type=jnp.float32) + bcv_ref[:, 0:dc]
    cvg = jnp.dot(h, wcv_ref[:, dc:2 * dc], preferred_element_type=jnp.float32) + bcv_ref[:, dc:2 * dc]
    glu = cva / (1.0 + jnp.exp(-cvg))
    ext_ref[:, pad:pad + l, :] = glu.reshape(gt, l, dc)

    def qkv_proj(idx, ref):
        lo, hi = idx * D_ATTN, (idx + 1) * D_ATTN
        if transposed:
            ref[0] = lax.dot_general(wqkv_ref[lo:hi, :], h, _NT,
                                     preferred_element_type=jnp.float32) + bqkv_ref[lo:hi, :]
        else:
            ref[...] = jnp.dot(h, wqkv_ref[:, lo:hi], preferred_element_type=jnp.float32) + bqkv_ref[:, lo:hi]

    rc = min(l, CONV_CHUNK)
    n_chunks = l // rc
    chunks = [(g, cidx * rc) for g in range(gt) for cidx in range(n_chunks)]
    pending = [(len(chunks) * (idx + 1) // 4, functools.partial(qkv_proj, idx, ref))
               for idx, ref in enumerate((q_ref, k_ref, v_ref))]
    pending += [(len(chunks) * kk // len(s_passes), one_pass) for kk, one_pass in enumerate(s_passes)]
    pending.sort(key=lambda item: item[0])
    for it in range(len(chunks) + 1):
        while pending and (it == len(chunks) or pending[0][0] <= it):
            pending.pop(0)[1]()
        if it == len(chunks):
            break
        g, r0 = chunks[it]
        for lt in range(dc // LANES):
            ls = slice(lt * LANES, (lt + 1) * LANES)
            win = ext_ref[g, r0:r0 + rc + pad, ls]
            shifted = {r: pltpu.roll(win, rc + pad - r, 0) for r in range(1, SUBLANES)}
            acc = None
            for j in range(width):
                a, r = divmod(h0 + j, SUBLANES)
                lo = a * SUBLANES
                tap = ext_ref[g, r0 + lo:r0 + lo + rc, ls] if r == 0 else shifted[r][lo:lo + rc]
                term = tap * wdw_ref[j:j + 1, ls]
                acc = term if acc is None else acc + term
            act_ref[g * l + r0:g * l + r0 + rc, ls] = acc + bdw_ref[:, ls]
        cv = act_ref[g * l + r0:g * l + r0 + rc, :]
        mu = jnp.mean(cv, axis=-1, keepdims=True)
        cen = cv - mu
        var = jnp.mean(cen * cen, axis=-1, keepdims=True)
        y = cen * lax.rsqrt(var + LN_EPS) * gln_ref[...] + bln_ref[...]
        act_ref[g * l + r0:g * l + r0 + rc, :] = _silu(y)

    c = jnp.dot(act_ref[...].astype(jnp.bfloat16), wpw_ref[...],
                preferred_element_type=jnp.float32) + bpw_ref[...]
    c_ref[...] = c.astype(c_ref.dtype)

    @pl.when(s == pl.num_programs(1) - 1)
    def _():
        cvs_ref[...] = ext_ref[:, l + h0:l + pad, :]

    ext_ref[:, 0:pad, :] = ext_ref[:, l:l + pad, :]

    if s_tail is not None:
        s_tail()


def _inproj_conv(x, hist, gpre, wqkv, bqkv, wcv, bcv, wdw, bdw, gln, bln, wpw, bpw, *, gt, l, transposed,
                 sample=None):
    g_total, s_total, d = x.shape
    n = g_total * s_total
    dc = wpw.shape[0]
    width = wdw.shape[0]
    pad = -(-(width - 1) // SUBLANES) * SUBLANES
    rows = gt * l
    n_s = s_total // l
    assert s_total % l == 0 and g_total % gt == 0 and l % SUBLANES == 0
    assert gt == 1 or n_s == 1
    grid = (g_total // gt, n_s)
    row_spec = lambda c: pl.BlockSpec((rows, c), lambda g, s, *_: (g * n_s + s, 0))
    hist_spec = pl.BlockSpec((gt, width - 1, dc), lambda g, s, *_: (g, 0, 0))
    f32 = jnp.float32
    if transposed:
        assert gt == 1
        qkv_spec = pl.BlockSpec((1, D_ATTN, l), lambda g, s, *_: (g, 0, s))
        qkv_shape = jax.ShapeDtypeStruct((g_total, D_ATTN, s_total), f32)
    else:
        qkv_spec = row_spec(D_ATTN)
        qkv_shape = jax.ShapeDtypeStruct((n, D_ATTN), f32)
    in_specs = [
        row_spec(d), hist_spec,
        _const_spec((1, d)), _const_spec(wqkv.shape), _const_spec(bqkv.shape),
        _const_spec(wcv.shape), _const_spec((1, 2 * dc)),
        _const_spec(wdw.shape), _const_spec((1, dc)), _const_spec((1, dc)), _const_spec((1, dc)),
        _const_spec(wpw.shape), _const_spec((1, dc)),
    ]
    out_specs = [qkv_spec, qkv_spec, qkv_spec, row_spec(dc), hist_spec]
    out_shape = [qkv_shape, qkv_shape, qkv_shape, jax.ShapeDtypeStruct((n, dc), jnp.bfloat16),
                 jax.ShapeDtypeStruct((g_total, width - 1, dc), f32)]
    scratch = [pltpu.VMEM((gt, pad + l, dc), f32), pltpu.VMEM((rows, dc), f32)]
    args = [x.reshape(n, d), hist, gpre.reshape(1, d), wqkv, bqkv, wcv, bcv.reshape(1, 2 * dc), wdw,
            bdw.reshape(1, dc), gln.reshape(1, dc), bln.reshape(1, dc), wpw, bpw.reshape(1, dc)]
    stream = None
    prefetch = []
    if sample is not None:
        qs, ks, vs, pool_k, pool_v, page_table, t = sample
        da = qs.shape[1]
        db, n_pages = page_table.shape
        n_phys, page = pool_k.shape[0], pool_k.shape[1]
        ppb = MOBA_BLOCK // page
        npb = (n_pages * page) // MOBA_BLOCK
        assert MOBA_BLOCK % page == 0 and n_pages == npb * ppb and npb >= 1 and npb <= LANES
        pb = PASS_BLOCKS
        slots = RING_PASSES * pb
        n_steps = grid[0] * grid[1]
        assert npb % pb == 0 and t % SUBLANES == 0 and page % LANES == 0
        total_pass = db * (npb // pb)
        assert total_pass % n_steps == 0 and total_pass >= RING_PASSES
        pps = total_pass // n_steps
        assert (npb // pb) % pps == 0
        spd = (npb // pb) // pps
        pk = jnp.transpose(pool_k, (0, 2, 3, 1)).reshape(n_phys, da, page)
        pv = jnp.transpose(pool_v, (0, 2, 3, 1)).reshape(n_phys, da, page)
        r = N_HEADS * t
        seq_spec = pl.BlockSpec((t, da), lambda g, s, *_: ((g * n_s + s) // spd, 0))
        hbm_spec = pl.BlockSpec(memory_space=pl.ANY)
        in_specs += [seq_spec, seq_spec, seq_spec, hbm_spec, hbm_spec]
        out_specs.append(seq_spec)
        out_shape.append(jax.ShapeDtypeStruct(qs.shape, jnp.bfloat16))
        scratch += [pltpu.VMEM((slots, ppb, da, page), f32),
                    pltpu.VMEM((slots, ppb, da, page), f32),
                    pltpu.SemaphoreType.DMA((slots, ppb)),
                    pltpu.SemaphoreType.DMA((slots, ppb)),
                    pltpu.VMEM((r, LANES), f32),
                    pltpu.VMEM((r, LANES), f32),
                    pltpu.VMEM((r, LANES), f32),
                    pltpu.VMEM((npb, r, LANES), f32)]
        args += [qs, ks, vs, pk, pv]
        prefetch = [page_table]
        stream = dict(pps=pps, dims=dict(t=t, npb=npb, ppb=ppb, pb=pb))
    grid_spec = pltpu.PrefetchScalarGridSpec(
        num_scalar_prefetch=len(prefetch), grid=grid, in_specs=in_specs, out_specs=out_specs,
        scratch_shapes=scratch)
    return pl.pallas_call(
        functools.partial(_inproj_conv_kernel, gt=gt, l=l, width=width, transposed=transposed, stream=stream),
        grid_spec=grid_spec,
        out_shape=out_shape,
        compiler_params=pltpu.CompilerParams(
            dimension_semantics=("arbitrary", "arbitrary"), vmem_limit_bytes=VMEM_LIMIT),
        name="inproj_conv",
    )(*prefetch, *args)


def _moba_prompt_kernel(qt_ref, kt_ref, vt_ref, o_ref, kaug_ref, vaug_ref, s_ref, p_ref, *, nb):
    blk = MOBA_BLOCK
    s_len = kt_ref.shape[2]
    qscale = HEAD_DIM ** -0.5 * LOG2E
    f32 = jnp.float32
    n_buf = s_ref.shape[0]

    kt = kt_ref[0]
    vt = vt_ref[0]
    key_blk = lax.broadcasted_iota(jnp.int32, (nb, s_len), 1) // blk
    ind = jnp.where(key_blk == lax.broadcasted_iota(jnp.int32, (nb, s_len), 0), 1.0 / blk, 0.0)
    km = lax.dot_general(ind, kt, _NT, precision=_HI, preferred_element_type=f32)
    km_lane = lax.broadcasted_iota(jnp.int32, (nb, LANES), 1)

    k_rows = kt.T
    row_blk = lax.broadcasted_iota(jnp.int32, (s_len, LANES), 0) // blk
    lane_s = lax.broadcasted_iota(jnp.int32, (s_len, LANES), 1)
    sub_s = lax.broadcasted_iota(jnp.int32, (LANES, s_len), 0)
    for hh in range(2):
        own_lo = hh * HEAD_DIM
        base = HEAD_DIM - own_lo
        own_s = (lane_s >= own_lo) & (lane_s < own_lo + HEAD_DIM)
        onehot = (lane_s - base == row_blk).astype(f32)
        kaug_ref[hh] = jnp.where(own_s, k_rows, onehot).astype(jnp.bfloat16)
        own_r = (sub_s >= own_lo) & (sub_s < own_lo + HEAD_DIM)
        vaug_ref[hh] = jnp.where(own_r, vt, jnp.where(sub_s == base, 1.0, 0.0)).astype(jnp.bfloat16)

    sub = lax.broadcasted_iota(jnp.int32, (nb, blk), 0)
    rr = lax.broadcasted_iota(jnp.int32, (blk, blk), 0)
    cc = lax.broadcasted_iota(jnp.int32, (blk, blk), 1)
    causal = rr <= cc
    zeros_fill = jnp.zeros((HEAD_DIM - nb, blk), f32)
    items = [(i, hh) for i in range(nb) for hh in range(2)]
    km_both = jnp.concatenate([jnp.where((km_lane >= hh * HEAD_DIM) & (km_lane < (hh + 1) * HEAD_DIM), km, 0.0)
                               for hh in range(2)], axis=0)
    gates = {}
    o_heads = {}

    def logits(n):
        i, hh = items[n]
        w = (i + 1) * blk
        slot = n % n_buf
        own_lo = hh * HEAD_DIM
        qt_i = qt_ref[0, :, i * blk:(i + 1) * blk]
        if hh == 0:
            gates[i] = jnp.dot(km_both, qt_i, precision=_HI, preferred_element_type=f32)
        gate = gates[i][hh * nb:(hh + 1) * nb]
        cnt = jnp.zeros((nb, blk), jnp.int32)
        for m in range(i):
            row = gate[m:m + 1, :]
            beats = (row > gate) | ((row == gate) & (m < sub))
            cnt = cnt + beats.astype(jnp.int32)
        keep = ((sub < i) & (cnt < MOBA_TOPK)) | (sub == i)
        bias = jnp.where(keep, 0.0, NEG)
        q_own = qt_i[own_lo:own_lo + HEAD_DIM, :] * qscale
        parts = [q_own, bias, zeros_fill] if hh == 0 else [bias, zeros_fill, q_own]
        q_aug = jnp.concatenate(parts, axis=0).astype(jnp.bfloat16)
        if i > 0:
            s_ref[slot, 0:w - blk, :] = jnp.dot(kaug_ref[hh, 0:w - blk, :], q_aug, preferred_element_type=f32)
        s_own = jnp.dot(kaug_ref[hh, w - blk:w, :], q_aug, preferred_element_type=f32)
        s_ref[slot, w - blk:w, :] = jnp.where(causal, s_own, NEG)

    def softmax(n):
        i, _ = items[n]
        w = (i + 1) * blk
        slot = n % n_buf
        sv = s_ref[slot, 0:w, :]
        mx = jnp.max(sv, axis=0, keepdims=True)
        p_ref[slot, 0:w, :] = jnp.exp2(sv - mx).astype(jnp.bfloat16)

    def values(n):
        i, hh = items[n]
        w = (i + 1) * blk
        slot = n % n_buf
        own_lo = hh * HEAD_DIM
        base = HEAD_DIM - own_lo
        o_t = jnp.dot(vaug_ref[hh, :, 0:w], p_ref[slot, 0:w, :], preferred_element_type=f32)
        o_heads[hh] = o_t[own_lo:own_lo + HEAD_DIM, :] * (1.0 / o_t[base:base + 1, :])
        if hh == 1:
            o_pair = jnp.concatenate([o_heads[0], o_heads[1]], axis=0)
            o_ref[0, i * blk:(i + 1) * blk, :] = o_pair.T.astype(o_ref.dtype)

    lag = (n_buf - 1) // 2
    for n in range(len(items) + 2 * lag):
        if n < len(items):
            logits(n)
        if 0 <= n - lag < len(items):
            softmax(n - lag)
        if 0 <= n - 2 * lag < len(items):
            values(n - 2 * lag)


def _moba_prompt(qt, kt, vt):
    b, da, s_len = qt.shape
    assert s_len % MOBA_BLOCK == 0 and da % LANES == 0 and LANES == 2 * HEAD_DIM
    nb = s_len // MOBA_BLOCK
    assert nb <= HEAD_DIM
    n_buf = 5
    spec = pl.BlockSpec((1, LANES, s_len), lambda bi, hp: (bi, hp, 0))
    return pl.pallas_call(
        functools.partial(_moba_prompt_kernel, nb=nb),
        grid=(b, da // LANES),
        in_specs=[spec, spec, spec],
        out_specs=pl.BlockSpec((1, s_len, LANES), lambda bi, hp: (bi, 0, hp)),
        out_shape=jax.ShapeDtypeStruct((b, s_len, da), jnp.bfloat16),
        scratch_shapes=[pltpu.VMEM((2, s_len, LANES), jnp.bfloat16),
                        pltpu.VMEM((2, LANES, s_len), jnp.bfloat16),
                        pltpu.VMEM((n_buf, s_len, MOBA_BLOCK), jnp.float32),
                        pltpu.VMEM((n_buf, s_len, MOBA_BLOCK), jnp.bfloat16)],
        compiler_params=pltpu.CompilerParams(
            dimension_semantics=("arbitrary", "arbitrary"), vmem_limit_bytes=VMEM_LIMIT),
        name="moba_prompt",
    )(qt, kt, vt)


def _fold_heads(o, row_head):
    r, da = o.shape
    lane_head = lax.broadcasted_iota(jnp.int32, (r, da), 1) // HEAD_DIM
    om = jnp.where(lane_head == row_head, o, 0.0)
    out = om[:, 0:LANES]
    for p in range(1, da // LANES):
        out = out + om[:, p * LANES:(p + 1) * LANES]
    return out


def _sample_stream(step, n_steps, pps, pt_ref, q_ref, kn_ref, vn_ref, pk_ref, pv_ref, o_ref,
                   kbuf, vbuf, ksem, vsem, g_ref, m_ref, l_ref, op_ref, *, t, npb, ppb, pb):
    r = N_HEADS * t
    da = q_ref.shape[-1]
    n_pass = npb // pb
    spd = n_pass // pps
    total_pass = n_steps * pps
    scale = HEAD_DIM ** -0.5 * LOG2E
    f32 = jnp.float32

    def pass_copies(gp, region):
        gp = jnp.minimum(gp, total_pass - 1)
        copies = []
        for bb in range(pb):
            g = gp * pb + bb
            slot = region * pb + bb
            for p in range(ppb):
                page = pt_ref[g // npb, (g % npb) * ppb + p]
                copies.append(pltpu.make_async_copy(pk_ref.at[page], kbuf.at[slot, p], ksem.at[slot, p]))
                copies.append(pltpu.make_async_copy(pv_ref.at[page], vbuf.at[slot, p], vsem.at[slot, p]))
        return copies

    def start_all(copies):
        for c in copies:
            c.start(priority=1)

    def head():
        @pl.when(step == 0)
        def _():
            for gp in range(RING_PASSES - 1):
                start_all(pass_copies(gp, gp))

        @pl.when(step % spd == 0)
        def _():
            g_ref[...] = jnp.zeros(g_ref.shape, g_ref.dtype)
            m_ref[...] = jnp.full(m_ref.shape, NEG, m_ref.dtype)
            l_ref[...] = jnp.zeros(l_ref.shape, l_ref.dtype)

    q = q_ref[...]
    q_rep = jnp.concatenate([q] * N_HEADS, axis=0)
    row_head = lax.broadcasted_iota(jnp.int32, (r, da), 0) // t
    lane_head = lax.broadcasted_iota(jnp.int32, (r, da), 1) // HEAD_DIM
    q_exp = jnp.where(row_head == lane_head, q_rep, 0.0)
    q_s = q_exp * scale
    q_log = q_s.astype(jnp.bfloat16)
    q_lo = (q_s - q_log.astype(f32)).astype(jnp.bfloat16)
    rg = (MXU_DIM // HEAD_DIM) * t
    groups = [(slice(a * rg, (a + 1) * rg), slice(a * MXU_DIM, (a + 1) * MXU_DIM)) for a in range(da // MXU_DIM)]
    q_two = [jnp.concatenate([q_log[rs, fs], q_lo[rs, fs]], axis=0) for rs, fs in groups]
    row_head_g = lax.broadcasted_iota(jnp.int32, (rg, MXU_DIM), 0) // t
    lane = lax.broadcasted_iota(jnp.int32, (r, LANES), 1)

    def one_pass(kk):
        gp = step * pps + kk
        n0 = (gp % n_pass) * pb
        region = gp % RING_PASSES
        for c in pass_copies(gp, region):
            c.wait()
        start_all(pass_copies(gp + RING_PASSES - 1, (gp + RING_PASSES - 1) % RING_PASSES))
        stats = {}

        def scores(bb):
            slot = region * pb + bb
            kb = jnp.concatenate([kbuf[slot, p].astype(jnp.bfloat16) for p in range(ppb)], axis=1)
            s_two = [jnp.dot(q_two[a], kb[fs, :], preferred_element_type=f32)
                     for a, (_, fs) in enumerate(groups)]
            sb = jnp.concatenate([s[0:rg] for s in s_two], axis=0)
            gb = jnp.sum(sb + jnp.concatenate([s[rg:2 * rg] for s in s_two], axis=0),
                         axis=-1, keepdims=True)
            mb = jnp.max(sb, axis=-1, keepdims=True)
            pf = jnp.exp2(sb - mb)
            lb = jnp.sum(pf, axis=-1, keepdims=True)
            stats[bb] = (gb, mb, lb, pf.astype(jnp.bfloat16))

        def values(bb):
            slot = region * pb + bb
            pbf = stats[bb][3]
            vb = jnp.concatenate([vbuf[slot, p].astype(jnp.bfloat16) for p in range(ppb)], axis=1)
            ob = [lax.dot_general(pbf[rs], vb[fs, :], _NT, preferred_element_type=f32)
                  for rs, fs in groups]
            op_ref[n0 + bb] = jnp.concatenate([_fold_heads(o, row_head_g) for o in ob], axis=0)

        skew = 2
        for bb in range(pb + skew):
            if bb < pb:
                scores(bb)
            if bb >= skew:
                values(bb - skew)
        g_new, m_new, l_new = g_ref[...], m_ref[...], l_ref[...]
        for bb in range(pb):
            gb, mb, lb, _ = stats[bb]
            g_new = jnp.where(lane == n0 + bb, gb, g_new)
            m_new = jnp.where(lane == n0 + bb, mb, m_new)
            l_new = jnp.where(lane == n0 + bb, lb, l_new)
        g_ref[...] = g_new
        m_ref[...] = m_new
        l_ref[...] = l_new

    def tail():
        @pl.when(step == n_steps - 1)
        def _():
            for j in range(RING_PASSES - 1):
                for c in pass_copies(total_pass + j, (total_pass + j) % RING_PASSES):
                    c.wait()

        @pl.when(step % spd == spd - 1)
        def _():
            sc = jnp.where(lane < npb, g_ref[...], -jnp.inf)
            sel = jnp.zeros((r, LANES), jnp.bool_)
            for _ in range(min(MOBA_TOPK, npb)):
                best = jnp.max(sc, axis=-1, keepdims=True)
                first = jnp.min(jnp.where(sc == best, lane, LANES), axis=-1, keepdims=True)
                hit = lane == first
                sel = sel | hit
                sc = jnp.where(hit, -jnp.inf, sc)

            kn = kn_ref[...]
            vn = vn_ref[...]
            so = lax.dot_general(q_log, kn.astype(jnp.bfloat16), _NT, preferred_element_type=f32)
            tq = lax.broadcasted_iota(jnp.int32, (r, t), 0) % t
            tk = lax.broadcasted_iota(jnp.int32, (r, t), 1)
            so = jnp.where(tk <= tq, so, NEG)
            m_own = jnp.max(so, axis=-1, keepdims=True)
            p_own = jnp.exp2(so - m_own)
            l_own = jnp.sum(p_own, axis=-1, keepdims=True)
            o_own = _fold_heads(jnp.dot(p_own.astype(jnp.bfloat16), vn.astype(jnp.bfloat16),
                                        preferred_element_type=f32), row_head)

            m_sel = jnp.where(sel, m_ref[...], NEG)
            m_tot = jnp.maximum(m_own, jnp.max(m_sel, axis=-1, keepdims=True))
            w = jnp.where(sel, jnp.exp2(m_sel - m_tot), 0.0)
            w_own = jnp.exp2(m_own - m_tot)
            l_tot = w_own * l_own + jnp.sum(w * l_ref[...], axis=-1, keepdims=True)

            def add_block(nn, acc):
                wn = jnp.sum(jnp.where(lane == nn, w, 0.0), axis=-1, keepdims=True)
                return acc + wn * op_ref[nn]

            o_tot = lax.fori_loop(0, npb, add_block, w_own * o_own) / l_tot
            tiles = []
            for p in range(da // LANES):
                lo = o_tot[(2 * p) * t:(2 * p + 1) * t, :]
                hi = o_tot[(2 * p + 1) * t:(2 * p + 2) * t, :]
                tiles.append(jnp.where(lane[0:t] < HEAD_DIM, lo, hi))
            o_ref[...] = jnp.concatenate(tiles, axis=-1).astype(o_ref.dtype)

    return head, [functools.partial(one_pass, kk) for kk in range(pps)], tail


def _out_ffn_kernel(x_ref, a_ref, c_ref, hist_ref, wo_ref, gpm_ref, gpf_ref, wup_ref, wdw_ref, bdw_ref,
                    wdn_ref, gpo_ref, y_ref, st_ref, carry_ref, f_ref, *, gt, l, width):
    s = pl.program_id(1)
    d_ff = wdn_ref.shape[0]
    da = a_ref.shape[-1]
    pad = carry_ref.shape[1]
    h0 = pad - (width - 1)

    @pl.when(s == 0)
    def _():
        carry_ref[:, h0:pad, :] = hist_ref[...]

    mix = (jnp.dot(a_ref[...], wo_ref[0:da, :], preferred_element_type=jnp.float32)
           + jnp.dot(c_ref[...], wo_ref[da:, :], preferred_element_type=jnp.float32))
    x1 = x_ref[...] + _rms(mix, gpm_ref[...])
    h = _rms(x1, gpf_ref[...]).astype(jnp.bfloat16)

    def up_proj(lo):
        return jnp.dot(h, wup_ref[:, lo:lo + FF_CHUNK], preferred_element_type=jnp.float32)

    def conv(u, lo):
        hi = lo + FF_CHUNK
        u3 = u.reshape(gt, l, FF_CHUNK)
        ext = jnp.concatenate([carry_ref[:, :, lo:hi], u3], axis=1)
        carry_ref[:, :, lo:hi] = u3[:, l - pad:, :]
        acc = u3 * wdw_ref[width - 1:width, lo:hi]
        for j in range(width - 1):
            back = width - 1 - j
            acc = acc + pltpu.roll(ext, back, 1)[:, pad:, :] * wdw_ref[j:j + 1, lo:hi]
        return (acc + bdw_ref[:, lo:hi]).reshape(gt * l, FF_CHUNK)

    n_cc = d_ff // FF_CHUNK
    ups = {}
    for cc in range(n_cc + 1):
        if cc < n_cc:
            ups[cc] = (up_proj(cc * FF_CHUNK), up_proj(d_ff + cc * FF_CHUNK))
        if cc >= 1:
            ug, uv = ups.pop(cc - 1)
            gate = conv(ug, (cc - 1) * FF_CHUNK)
            val = conv(uv, d_ff + (cc - 1) * FF_CHUNK)
            f_ref[:, (cc - 1) * FF_CHUNK:cc * FF_CHUNK] = (_silu(gate) * val).astype(jnp.bfloat16)

    ff = jnp.dot(f_ref[...], wdn_ref[...], preferred_element_type=jnp.float32)
    y_ref[...] = x1 + _rms(ff, gpo_ref[...])

    @pl.when(s == pl.num_programs(1) - 1)
    def _():
        st_ref[...] = carry_ref[:, h0:pad, :]


def _out_ffn(x, attn, c, hist, wo, gpm, gpf, wup, wdw, bdw, wdn, gpo, *, gt, l):
    g_total, s_total, d = x.shape
    n = g_total * s_total
    d_ff = wdn.shape[0]
    width = wdw.shape[0]
    pad = SUBLANES
    assert width - 1 <= pad <= l and d_ff % FF_CHUNK == 0
    rows = gt * l
    n_s = s_total // l
    assert s_total % l == 0 and g_total % gt == 0 and l % SUBLANES == 0
    assert gt == 1 or n_s == 1
    row_spec = lambda cdim: pl.BlockSpec((rows, cdim), lambda g, s: (g * n_s + s, 0))
    st_spec = pl.BlockSpec((gt, width - 1, 2 * d_ff), lambda g, s: (g, 0, 0))
    f32 = jnp.float32
    return pl.pallas_call(
        functools.partial(_out_ffn_kernel, gt=gt, l=l, width=width),
        grid=(g_total // gt, n_s),
        in_specs=[
            row_spec(d), row_spec(attn.shape[1]), row_spec(c.shape[1]), st_spec,
            _const_spec(wo.shape), _const_spec((1, d)), _const_spec((1, d)),
            _const_spec(wup.shape), _const_spec(wdw.shape), _const_spec((1, 2 * d_ff)),
            _const_spec(wdn.shape), _const_spec((1, d)),
        ],
        out_specs=[row_spec(d), st_spec],
        out_shape=[jax.ShapeDtypeStruct((n, d), f32),
                   jax.ShapeDtypeStruct((g_total, width - 1, 2 * d_ff), f32)],
        scratch_shapes=[pltpu.VMEM((gt, pad, 2 * d_ff), f32),
                        pltpu.VMEM((rows, d_ff), jnp.bfloat16)],
        compiler_params=pltpu.CompilerParams(
            dimension_semantics=("arbitrary", "arbitrary"), vmem_limit_bytes=VMEM_LIMIT),
        name="out_ffn",
    )(x.reshape(n, d), attn, c, hist, wo, gpm.reshape(1, d), gpf.reshape(1, d), wup, wdw,
      bdw.reshape(1, -1), wdn, gpo.reshape(1, d))


def kernel(x_prompt, x_sample, cache_k, cache_v, page_table, state_cv, state_ffn, g_pre_mix, w_in, b_in,
           w_cv_dw, b_cv_dw, g_cv_ln, b_cv_ln, w_cv_pw, b_cv_pw, w_out, g_post_mix, g_pre_ffn, w_up,
           w_ffn_dw, b_ffn_dw, w_down, g_post_ffn):
    depth = w_in.shape[0]
    bf16 = jnp.bfloat16
    b, s_len, d = x_prompt.shape
    db, t, _ = x_sample.shape
    n_qkv = 3 * D_ATTN
    kv_s = (db, t, N_HEADS, HEAD_DIM)
    yp, ys = x_prompt, x_sample
    outs = [[] for _ in range(8)]
    for lyr in range(depth):
        w_qkv = w_in[lyr][:, :n_qkv].astype(bf16)
        b_qkv = b_in[lyr][:n_qkv]
        cv_prm = (w_in[lyr][:, n_qkv:].astype(bf16), b_in[lyr][n_qkv:], w_cv_dw[lyr], b_cv_dw[lyr],
                  g_cv_ln[lyr], b_cv_ln[lyr], w_cv_pw[lyr].astype(bf16), b_cv_pw[lyr])
        ffn_prm = (w_out[lyr].astype(bf16), g_post_mix[lyr], g_pre_ffn[lyr], w_up[lyr].astype(bf16),
                   w_ffn_dw[lyr], b_ffn_dw[lyr], w_down[lyr].astype(bf16), g_post_ffn[lyr])

        q, k, v, c_s, cvs = _inproj_conv(ys, state_cv[lyr], g_pre_mix[lyr], w_qkv, b_qkv.reshape(1, n_qkv), *cv_prm,
                                         gt=db, l=t, transposed=False)

        cv0 = jnp.zeros((b,) + state_cv.shape[2:], x_prompt.dtype)
        ff0 = jnp.zeros((b,) + state_ffn.shape[2:], x_prompt.dtype)
        lp = min(ROW_TILE, s_len)
        qt, kt, vt, c, cvp, attn_s = _inproj_conv(
            yp, cv0, g_pre_mix[lyr], w_qkv.T, b_qkv.reshape(n_qkv, 1), *cv_prm, gt=1, l=lp, transposed=True,
            sample=(q, k, v, cache_k[lyr], cache_v[lyr], page_table, t))
        attn = _moba_prompt(qt, kt, vt).reshape(b * s_len, D_ATTN)
        yp, ffp = _out_ffn(yp, attn, c, ff0, *ffn_prm, gt=1, l=lp)
        yp = yp.reshape(b, s_len, d)
        kp = jnp.transpose(kt.reshape(b, N_HEADS, HEAD_DIM, s_len), (0, 3, 1, 2))
        vp = jnp.transpose(vt.reshape(b, N_HEADS, HEAD_DIM, s_len), (0, 3, 1, 2))

        ys, ffs = _out_ffn(ys, attn_s, c_s, state_ffn[lyr], *ffn_prm, gt=db, l=t)
        ys = ys.reshape(db, t, d)
        for lst, val in zip(outs, (kp, vp, cvp, ffp, k.reshape(kv_s), v.reshape(kv_s), cvs, ffs)):
            lst.append(val)
    return (yp, ys) + tuple(jnp.stack(lst) for lst in outs)
```

```python
import functools

import jax
import jax.numpy as jnp
from jax import lax
from jax.experimental import pallas as pl
from jax.experimental.pallas import tpu as pltpu

N_HEADS = 8
HEAD_DIM = 64
D_ATTN = N_HEADS * HEAD_DIM
MOBA_BLOCK = 256
MOBA_TOPK = 3
RMS_EPS = 1e-6
LN_EPS = 1e-5
NEG = -1e30
LOG2E = 1.4426950408889634

LANES = 128
SUBLANES = 8
MXU_DIM = 256
VMEM_LIMIT = 56 * 1024 * 1024

ROW_TILE = 512
CONV_CHUNK = 32
FF_CHUNK = 256
PASS_BLOCKS = 8
RING_PASSES = 4

_NT = (((1,), (1,)), ((), ()))
_HI = lax.Precision.HIGHEST


def _const_spec(shape):
    nd = len(shape)
    return pl.BlockSpec(shape, lambda *_: (0,) * nd, pipeline_mode=pl.Buffered(1))


def _rms(x, g):
    return x * lax.rsqrt(jnp.mean(x * x, axis=-1, keepdims=True) + RMS_EPS) * g


def _silu(x):
    return x / (1.0 + jnp.exp(-x))


def _inproj_conv_kernel(*refs, gt, l, width, transposed, stream):
    if stream is not None:
        pt_ref, refs = refs[0], refs[1:]
    (x_ref, hist_ref, gpre_ref, wqkv_ref, bqkv_ref, wcv_ref, bcv_ref, wdw_ref, bdw_ref,
     gln_ref, bln_ref, wpw_ref, bpw_ref) = refs[:13]
    refs = refs[13:]
    if stream is not None:
        stream_in, refs = refs[:5], refs[5:]
    q_ref, k_ref, v_ref, c_ref, cvs_ref = refs[:5]
    refs = refs[5:]
    if stream is not None:
        stream_out, refs = refs[0], refs[1:]
    ext_ref, act_ref = refs[:2]
    s = pl.program_id(1)
    if stream is not None:
        n_steps = pl.num_programs(0) * pl.num_programs(1)
        step = pl.program_id(0) * pl.num_programs(1) + s
        s_head, s_passes, s_tail = _sample_stream(step, n_steps, stream["pps"], pt_ref, *stream_in, stream_out,
                                                  *refs[2:], **stream["dims"])
        s_head()
    else:
        s_passes, s_tail = [], None
    dc = act_ref.shape[-1]
    pad = ext_ref.shape[1] - l
    h0 = pad - (width - 1)

    @pl.when(s == 0)
    def _():
        ext_ref[:, h0:pad, :] = hist_ref[...]

    h = _rms(x_ref[...], gpre_ref[...]).astype(jnp.bfloat16)

    cva = jnp.dot(h, wcv_ref[:, 0:dc], preferred_element_type=jnp.float32) + bcv_ref[:, 0:dc]
    cvg = jnp.dot(h, wcv_ref[:, dc:2 * dc], preferred_element_type=jnp.float32) + bcv_ref[:, dc:2 * dc]
    glu = cva / (1.0 + jnp.exp(-cvg))
    ext_ref[:, pad:pad + l, :] = glu.reshape(gt, l, dc)

    def qkv_proj(idx, ref):
        lo, hi = idx * D_ATTN, (idx + 1) * D_ATTN
        if transposed:
            ref[0] = lax.dot_general(wqkv_ref[lo:hi, :], h, _NT,
                                     preferred_element_type=jnp.float32) + bqkv_ref[lo:hi, :]
        else:
            ref[...] = jnp.dot(h, wqkv_ref[:, lo:hi], preferred_element_type=jnp.float32) + bqkv_ref[:, lo:hi]

    rc = min(l, CONV_CHUNK)
    n_chunks = l // rc
    chunks = [(g, cidx * rc) for g in range(gt) for cidx in range(n_chunks)]
    pending = [(len(chunks) * (idx + 1) // 4, functools.partial(qkv_proj, idx, ref))
               for idx, ref in enumerate((q_ref, k_ref, v_ref))]
    pending += [(len(chunks) * kk // len(s_passes), one_pass) for kk, one_pass in enumerate(s_passes)]
    pending.sort(key=lambda item: item[0])
    for it in range(len(chunks) + 1):
        while pending and (it == len(chunks) or pending[0][0] <= it):
            pending.pop(0)[1]()
        if it == len(chunks):
            break
        g, r0 = chunks[it]
        for lt in range(dc // LANES):
            ls = slice(lt * LANES, (lt + 1) * LANES)
            win = ext_ref[g, r0:r0 + rc + pad, ls]
            shifted = {r: pltpu.roll(win, rc + pad - r, 0) for r in range(1, SUBLANES)}
            acc = None
            for j in range(width):
                a, r = divmod(h0 + j, SUBLANES)
                lo = a * SUBLANES
                tap = ext_ref[g, r0 + lo:r0 + lo + rc, ls] if r == 0 else shifted[r][lo:lo + rc]
                term = tap * wdw_ref[j:j + 1, ls]
                acc = term if acc is None else acc + term
            act_ref[g * l + r0:g * l + r0 + rc, ls] = acc + bdw_ref[:, ls]
        cv = act_ref[g * l + r0:g * l + r0 + rc, :]
        mu = jnp.mean(cv, axis=-1, keepdims=True)
        cen = cv - mu
        var = jnp.mean(cen * cen, axis=-1, keepdims=True)
        y = cen * lax.rsqrt(var + LN_EPS) * gln_ref[...] + bln_ref[...]
        act_ref[g * l + r0:g * l + r0 + rc, :] = _silu(y)

    c = jnp.dot(act_ref[...].astype(jnp.bfloat16), wpw_ref[...],
                preferred_element_type=jnp.float32) + bpw_ref[...]
    c_ref[...] = c.astype(c_ref.dtype)

    @pl.when(s == pl.num_programs(1) - 1)
    def _():
        cvs_ref[...] = ext_ref[:, l + h0:l + pad, :]

    ext_ref[:, 0:pad, :] = ext_ref[:, l:l + pad, :]

    if s_tail is not None:
        s_tail()


def _inproj_conv(x, hist, gpre, wqkv, bqkv, wcv, bcv, wdw, bdw, gln, bln, wpw, bpw, *, gt, l, transposed,
                 sample=None):
    g_total, s_total, d = x.shape
    n = g_total * s_total
    dc = wpw.shape[0]
    width = wdw.shape[0]
    pad = -(-(width - 1) // SUBLANES) * SUBLANES
    rows = gt * l
    n_s = s_total // l
    assert s_total % l == 0 and g_total % gt == 0 and l % SUBLANES == 0
    assert gt == 1 or n_s == 1
    grid = (g_total // gt, n_s)
    row_spec = lambda c: pl.BlockSpec((rows, c), lambda g, s, *_: (g * n_s + s, 0))
    hist_spec = pl.BlockSpec((gt, width - 1, dc), lambda g, s, *_: (g, 0, 0))
    f32 = jnp.float32
    if transposed:
        assert gt == 1
        qkv_spec = pl.BlockSpec((1, D_ATTN, l), lambda g, s, *_: (g, 0, s))
        qkv_shape = jax.ShapeDtypeStruct((g_total, D_ATTN, s_total), f32)
    else:
        qkv_spec = row_spec(D_ATTN)
        qkv_shape = jax.ShapeDtypeStruct((n, D_ATTN), f32)
    in_specs = [
        row_spec(d), hist_spec,
        _const_spec((1, d)), _const_spec(wqkv.shape), _const_spec(bqkv.shape),
        _const_spec(wcv.shape), _const_spec((1, 2 * dc)),
        _const_spec(wdw.shape), _const_spec((1, dc)), _const_spec((1, dc)), _const_spec((1, dc)),
        _const_spec(wpw.shape), _const_spec((1, dc)),
    ]
    out_specs = [qkv_spec, qkv_spec, qkv_spec, row_spec(dc), hist_spec]
    out_shape = [qkv_shape, qkv_shape, qkv_shape, jax.ShapeDtypeStruct((n, dc), jnp.bfloat16),
                 jax.ShapeDtypeStruct((g_total, width - 1, dc), f32)]
    scratch = [pltpu.VMEM((gt, pad + l, dc), f32), pltpu.VMEM((rows, dc), f32)]
    args = [x.reshape(n, d), hist, gpre.reshape(1, d), wqkv, bqkv, wcv, bcv.reshape(1, 2 * dc), wdw,
            bdw.reshape(1, dc), gln.reshape(1, dc), bln.reshape(1, dc), wpw, bpw.reshape(1, dc)]
    stream = None
    prefetch = []
    if sample is not None:
        qs, ks, vs, pool_k, pool_v, page_table, t = sample
        da = qs.shape[1]
        db, n_pages = page_table.shape
        n_phys, page = pool_k.shape[0], pool_k.shape[1]
        ppb = MOBA_BLOCK // page
        npb = (n_pages * page) // MOBA_BLOCK
        assert MOBA_BLOCK % page == 0 and n_pages == npb * ppb and npb >= 1 and npb <= LANES
        pb = PASS_BLOCKS
        slots = RING_PASSES * pb
        n_steps = grid[0] * grid[1]
        assert npb % pb == 0 and t % SUBLANES == 0 and page % LANES == 0
        total_pass = db * (npb // pb)
        assert total_pass % n_steps == 0 and total_pass >= RING_PASSES
        pps = total_pass // n_steps
        assert (npb // pb) % pps == 0
        spd = (npb // pb) // pps
        pk = jnp.transpose(pool_k, (0, 2, 3, 1)).reshape(n_phys, da, page)
        pv = jnp.transpose(pool_v, (0, 2, 3, 1)).reshape(n_phys, da, page)
        r = N_HEADS * t
        seq_spec = pl.BlockSpec((t, da), lambda g, s, *_: ((g * n_s + s) // spd, 0))
        hbm_spec = pl.BlockSpec(memory_space=pl.ANY)
        in_specs += [seq_spec, seq_spec, seq_spec, hbm_spec, hbm_spec]
        out_specs.append(seq_spec)
        out_shape.append(jax.ShapeDtypeStruct(qs.shape, jnp.bfloat16))
        scratch += [pltpu.VMEM((slots, ppb, da, page), f32),
                    pltpu.VMEM((slots, ppb, da, page), f32),
                    pltpu.SemaphoreType.DMA((slots, ppb)),
                    pltpu.SemaphoreType.DMA((slots, ppb)),
                    pltpu.VMEM((r, LANES), f32),
                    pltpu.VMEM((r, LANES), f32),
                    pltpu.VMEM((r, LANES), f32),
                    pltpu.VMEM((npb, r, LANES), f32)]
        args += [qs, ks, vs, pk, pv]
        prefetch = [page_table]
        stream = dict(pps=pps, dims=dict(t=t, npb=npb, ppb=ppb, pb=pb))
    grid_spec = pltpu.PrefetchScalarGridSpec(
        num_scalar_prefetch=len(prefetch), grid=grid, in_specs=in_specs, out_specs=out_specs,
        scratch_shapes=scratch)
    return pl.pallas_call(
        functools.partial(_inproj_conv_kernel, gt=gt, l=l, width=width, transposed=transposed, stream=stream),
        grid_spec=grid_spec,
        out_shape=out_shape,
        compiler_params=pltpu.CompilerParams(
            dimension_semantics=("arbitrary", "arbitrary"), vmem_limit_bytes=VMEM_LIMIT),
        name="inproj_conv",
    )(*prefetch, *args)


def _moba_prompt_kernel(qt_ref, kt_ref, vt_ref, o_ref, kaug_ref, vaug_ref, s_ref, p_ref, *, nb):
    blk = MOBA_BLOCK
    s_len = kt_ref.shape[2]
    qscale = HEAD_DIM ** -0.5 * LOG2E
    f32 = jnp.float32
    n_buf = s_ref.shape[0]

    kt = kt_ref[0]
    vt = vt_ref[0]
    key_blk = lax.broadcasted_iota(jnp.int32, (nb, s_len), 1) // blk
    ind = jnp.where(key_blk == lax.broadcasted_iota(jnp.int32, (nb, s_len), 0), 1.0 / blk, 0.0)
    km = lax.dot_general(ind, kt, _NT, precision=_HI, preferred_element_type=f32)
    km_lane = lax.broadcasted_iota(jnp.int32, (nb, LANES), 1)

    k_rows = kt.T
    row_blk = lax.broadcasted_iota(jnp.int32, (s_len, LANES), 0) // blk
    lane_s = lax.broadcasted_iota(jnp.int32, (s_len, LANES), 1)
    sub_s = lax.broadcasted_iota(jnp.int32, (LANES, s_len), 0)
    for hh in range(2):
        own_lo = hh * HEAD_DIM
        base = HEAD_DIM - own_lo
        own_s = (lane_s >= own_lo) & (lane_s < own_lo + HEAD_DIM)
        onehot = (lane_s - base == row_blk).astype(f32)
        kaug_ref[hh] = jnp.where(own_s, k_rows, onehot).astype(jnp.bfloat16)
        own_r = (sub_s >= own_lo) & (sub_s < own_lo + HEAD_DIM)
        vaug_ref[hh] = jnp.where(own_r, vt, jnp.where(sub_s == base, 1.0, 0.0)).astype(jnp.bfloat16)

    sub = lax.broadcasted_iota(jnp.int32, (nb, blk), 0)
    rr = lax.broadcasted_iota(jnp.int32, (blk, blk), 0)
    cc = lax.broadcasted_iota(jnp.int32, (blk, blk), 1)
    causal = rr <= cc
    zeros_fill = jnp.zeros((HEAD_DIM - nb, blk), f32)
    items = [(i, hh) for i in range(nb) for hh in range(2)]
    km_both = jnp.concatenate([jnp.where((km_lane >= hh * HEAD_DIM) & (km_lane < (hh + 1) * HEAD_DIM), km, 0.0)
                               for hh in range(2)], axis=0)
    gates = {}
    o_heads = {}

    def logits(n):
        i, hh = items[n]
        w = (i + 1) * blk
        slot = n % n_buf
        own_lo = hh * HEAD_DIM
        qt_i = qt_ref[0, :, i * blk:(i + 1) * blk]
        if hh == 0:
            gates[i] = jnp.dot(km_both, qt_i, precision=_HI, preferred_element_type=f32)
        gate = gates[i][hh * nb:(hh + 1) * nb]
        cnt = jnp.zeros((nb, blk), jnp.int32)
        for m in range(i):
            row = gate[m:m + 1, :]
            beats = (row > gate) | ((row == gate) & (m < sub))
            cnt = cnt + beats.astype(jnp.int32)
        keep = ((sub < i) & (cnt < MOBA_TOPK)) | (sub == i)
        bias = jnp.where(keep, 0.0, NEG)
        q_own = qt_i[own_lo:own_lo + HEAD_DIM, :] * qscale
        parts = [q_own, bias, zeros_fill] if hh == 0 else [bias, zeros_fill, q_own]
        q_aug = jnp.concatenate(parts, axis=0).astype(jnp.bfloat16)
        if i > 0:
            s_ref[slot, 0:w - blk, :] = jnp.dot(kaug_ref[hh, 0:w - blk, :], q_aug, preferred_element_type=f32)
        s_own = jnp.dot(kaug_ref[hh, w - blk:w, :], q_aug, preferred_element_type=f32)
        s_ref[slot, w - blk:w, :] = jnp.where(causal, s_own, NEG)

    def softmax(n):
        i, _ = items[n]
        w = (i + 1) * blk
        slot = n % n_buf
        sv = s_ref[slot, 0:w, :]
        mx = jnp.max(sv, axis=0, keepdims=True)
        p_ref[slot, 0:w, :] = jnp.exp2(sv - mx).astype(jnp.bfloat16)

    def values(n):
        i, hh = items[n]
        w = (i + 1) * blk
        slot = n % n_buf
        own_lo = hh * HEAD_DIM
        base = HEAD_DIM - own_lo
        o_t = jnp.dot(vaug_ref[hh, :, 0:w], p_ref[slot, 0:w, :], preferred_element_type=f32)
        o_heads[hh] = o_t[own_lo:own_lo + HEAD_DIM, :] * (1.0 / o_t[base:base + 1, :])
        if hh == 1:
            o_pair = jnp.concatenate([o_heads[0], o_heads[1]], axis=0)
            o_ref[0, i * blk:(i + 1) * blk, :] = o_pair.T.astype(o_ref.dtype)

    lag = (n_buf - 1) // 2
    for n in range(len(items) + 2 * lag):
        if n < len(items):
            logits(n)
        if 0 <= n - lag < len(items):
            softmax(n - lag)
        if 0 <= n - 2 * lag < len(items):
            values(n - 2 * lag)


def _moba_prompt(qt, kt, vt):
    b, da, s_len = qt.shape
    assert s_len % MOBA_BLOCK == 0 and da % LANES == 0 and LANES == 2 * HEAD_DIM
    nb = s_len // MOBA_BLOCK
    assert nb <= HEAD_DIM
    n_buf = 5
    spec = pl.BlockSpec((1, LANES, s_len), lambda bi, hp: (bi, hp, 0))
    return pl.pallas_call(
        functools.partial(_moba_prompt_kernel, nb=nb),
        grid=(b, da // LANES),
        in_specs=[spec, spec, spec],
        out_specs=pl.BlockSpec((1, s_len, LANES), lambda bi, hp: (bi, 0, hp)),
        out_shape=jax.ShapeDtypeStruct((b, s_len, da), jnp.bfloat16),
        scratch_shapes=[pltpu.VMEM((2, s_len, LANES), jnp.bfloat16),
                        pltpu.VMEM((2, LANES, s_len), jnp.bfloat16),
                        pltpu.VMEM((n_buf, s_len, MOBA_BLOCK), jnp.float32),
                        pltpu.VMEM((n_buf, s_len, MOBA_BLOCK), jnp.bfloat16)],
        compiler_params=pltpu.CompilerParams(
            dimension_semantics=("arbitrary", "arbitrary"), vmem_limit_bytes=VMEM_LIMIT),
        name="moba_prompt",
    )(qt, kt, vt)


def _fold_heads(o, row_head):
    r, da = o.shape
    lane_head = lax.broadcasted_iota(jnp.int32, (r, da), 1) // HEAD_DIM
    om = jnp.where(lane_head == row_head, o, 0.0)
    out = om[:, 0:LANES]
    for p in range(1, da // LANES):
        out = out + om[:, p * LANES:(p + 1) * LANES]
    return out


def _sample_stream(step, n_steps, pps, pt_ref, q_ref, kn_ref, vn_ref, pk_ref, pv_ref, o_ref,
                   kbuf, vbuf, ksem, vsem, g_ref, m_ref, l_ref, op_ref, *, t, npb, ppb, pb):
    r = N_HEADS * t
    da = q_ref.shape[-1]
    n_pass = npb // pb
    spd = n_pass // pps
    total_pass = n_steps * pps
    scale = HEAD_DIM ** -0.5 * LOG2E
    f32 = jnp.float32

    def pass_copies(gp, region):
        gp = jnp.minimum(gp, total_pass - 1)
        copies = []
        for bb in range(pb):
            g = gp * pb + bb
            slot = region * pb + bb
            for p in range(ppb):
                page = pt_ref[g // npb, (g % npb) * ppb + p]
                copies.append(pltpu.make_async_copy(pk_ref.at[page], kbuf.at[slot, p], ksem.at[slot, p]))
                copies.append(pltpu.make_async_copy(pv_ref.at[page], vbuf.at[slot, p], vsem.at[slot, p]))
        return copies

    def start_all(copies):
        for c in copies:
            c.start(priority=1)

    def head():
        @pl.when(step == 0)
        def _():
            for gp in range(RING_PASSES - 1):
                start_all(pass_copies(gp, gp))

        @pl.when(step % spd == 0)
        def _():
            g_ref[...] = jnp.zeros(g_ref.shape, g_ref.dtype)
            m_ref[...] = jnp.full(m_ref.shape, NEG, m_ref.dtype)
            l_ref[...] = jnp.zeros(l_ref.shape, l_ref.dtype)

    q = q_ref[...]
    q_rep = jnp.concatenate([q] * N_HEADS, axis=0)
    row_head = lax.broadcasted_iota(jnp.int32, (r, da), 0) // t
    lane_head = lax.broadcasted_iota(jnp.int32, (r, da), 1) // HEAD_DIM
    q_exp = jnp.where(row_head == lane_head, q_rep, 0.0)
    q_s = q_exp * scale
    q_log = q_s.astype(jnp.bfloat16)
    q_lo = (q_s - q_log.astype(f32)).astype(jnp.bfloat16)
    rg = (MXU_DIM // HEAD_DIM) * t
    groups = [(slice(a * rg, (a + 1) * rg), slice(a * MXU_DIM, (a + 1) * MXU_DIM)) for a in range(da // MXU_DIM)]
    q_two = [jnp.concatenate([q_log[rs, fs], q_lo[rs, fs]], axis=0) for rs, fs in groups]
    row_head_g = lax.broadcasted_iota(jnp.int32, (rg, MXU_DIM), 0) // t
    lane = lax.broadcasted_iota(jnp.int32, (r, LANES), 1)

    def one_pass(kk):
        gp = step * pps + kk
        n0 = (gp % n_pass) * pb
        region = gp % RING_PASSES
        for c in pass_copies(gp, region):
            c.wait()
        start_all(pass_copies(gp + RING_PASSES - 1, (gp + RING_PASSES - 1) % RING_PASSES))
        stats = {}

        def scores(bb):
            slot = region * pb + bb
            kb = jnp.concatenate([kbuf[slot, p].astype(jnp.bfloat16) for p in range(ppb)], axis=1)
            s_two = [jnp.dot(q_two[a], kb[fs, :], preferred_element_type=f32)
                     for a, (_, fs) in enumerate(groups)]
            sb = jnp.concatenate([s[0:rg] for s in s_two], axis=0)
            gb = jnp.sum(sb + jnp.concatenate([s[rg:2 * rg] for s in s_two], axis=0),
                         axis=-1, keepdims=True)
            mb = jnp.max(sb, axis=-1, keepdims=True)
            pf = jnp.exp2(sb - mb)
            lb = jnp.sum(pf, axis=-1, keepdims=True)
            stats[bb] = (gb, mb, lb, pf.astype(jnp.bfloat16))

        def values(bb):
            slot = region * pb + bb
            pbf = stats[bb][3]
            vb = jnp.concatenate([vbuf[slot, p].astype(jnp.bfloat16) for p in range(ppb)], axis=1)
            ob = [lax.dot_general(pbf[rs], vb[fs, :], _NT, preferred_element_type=f32)
                  for rs, fs in groups]
            op_ref[n0 + bb] = jnp.concatenate([_fold_heads(o, row_head_g) for o in ob], axis=0)

        skew = 2
        for bb in range(pb + skew):
            if bb < pb:
                scores(bb)
            if bb >= skew:
                values(bb - skew)
        g_new, m_new, l_new = g_ref[...], m_ref[...], l_ref[...]
        for bb in range(pb):
            gb, mb, lb, _ = stats[bb]
            g_new = jnp.where(lane == n0 + bb, gb, g_new)
            m_new = jnp.where(lane == n0 + bb, mb, m_new)
            l_new = jnp.where(lane == n0 + bb, lb, l_new)
        g_ref[...] = g_new
        m_ref[...] = m_new
        l_ref[...] = l_new

    def tail():
        @pl.when(step == n_steps - 1)
        def _():
            for j in range(RING_PASSES - 1):
                for c in pass_copies(total_pass + j, (total_pass + j) % RING_PASSES):
                    c.wait()

        @pl.when(step % spd == spd - 1)
        def _():
            sc = jnp.where(lane < npb, g_ref[...], -jnp.inf)
            sel = jnp.zeros((r, LANES), jnp.bool_)
            for _ in range(min(MOBA_TOPK, npb)):
                best = jnp.max(sc, axis=-1, keepdims=True)
                first = jnp.min(jnp.where(sc == best, lane, LANES), axis=-1, keepdims=True)
                hit = lane == first
                sel = sel | hit
                sc = jnp.where(hit, -jnp.inf, sc)

            kn = kn_ref[...]
            vn = vn_ref[...]
            so = lax.dot_general(q_log, kn.astype(jnp.bfloat16), _NT, preferred_element_type=f32)
            tq = lax.broadcasted_iota(jnp.int32, (r, t), 0) % t
            tk = lax.broadcasted_iota(jnp.int32, (r, t), 1)
            so = jnp.where(tk <= tq, so, NEG)
            m_own = jnp.max(so, axis=-1, keepdims=True)
            p_own = jnp.exp2(so - m_own)
            l_own = jnp.sum(p_own, axis=-1, keepdims=True)
            o_own = _fold_heads(jnp.dot(p_own.astype(jnp.bfloat16), vn.astype(jnp.bfloat16),
                                        preferred_element_type=f32), row_head)

            m_sel = jnp.where(sel, m_ref[...], NEG)
            m_tot = jnp.maximum(m_own, jnp.max(m_sel, axis=-1, keepdims=True))
            w = jnp.where(sel, jnp.exp2(m_sel - m_tot), 0.0)
            w_own = jnp.exp2(m_own - m_tot)
            l_tot = w_own * l_own + jnp.sum(w * l_ref[...], axis=-1, keepdims=True)

            def add_block(nn, acc):
                wn = jnp.sum(jnp.where(lane == nn, w, 0.0), axis=-1, keepdims=True)
                return acc + wn * op_ref[nn]

            o_tot = lax.fori_loop(0, npb, add_block, w_own * o_own) / l_tot
            tiles = []
            for p in range(da // LANES):
                lo = o_tot[(2 * p) * t:(2 * p + 1) * t, :]
                hi = o_tot[(2 * p + 1) * t:(2 * p + 2) * t, :]
                tiles.append(jnp.where(lane[0:t] < HEAD_DIM, lo, hi))
            o_ref[...] = jnp.concatenate(tiles, axis=-1).astype(o_ref.dtype)

    return head, [functools.partial(one_pass, kk) for kk in range(pps)], tail


def _out_ffn_kernel(x_ref, a_ref, c_ref, hist_ref, wo_ref, gpm_ref, gpf_ref, wup_ref, wdw_ref, bdw_ref,
                    wdn_ref, gpo_ref, y_ref, st_ref, carry_ref, f_ref, *, gt, l, width):
    s = pl.program_id(1)
    d_ff = wdn_ref.shape[0]
    da = a_ref.shape[-1]
    pad = carry_ref.shape[1]
    h0 = pad - (width - 1)

    @pl.when(s == 0)
    def _():
        carry_ref[:, h0:pad, :] = hist_ref[...]

    mix = (jnp.dot(a_ref[...], wo_ref[0:da, :], preferred_element_type=jnp.float32)
           + jnp.dot(c_ref[...], wo_ref[da:, :], preferred_element_type=jnp.float32))
    x1 = x_ref[...] + _rms(mix, gpm_ref[...])
    h = _rms(x1, gpf_ref[...]).astype(jnp.bfloat16)

    def up_proj(lo):
        return jnp.dot(h, wup_ref[:, lo:lo + FF_CHUNK], preferred_element_type=jnp.float32)

    def conv(u, lo):
        hi = lo + FF_CHUNK
        u3 = u.reshape(gt, l, FF_CHUNK)
        ext = jnp.concatenate([carry_ref[:, :, lo:hi], u3], axis=1)
        carry_ref[:, :, lo:hi] = u3[:, l - pad:, :]
        acc = u3 * wdw_ref[width - 1:width, lo:hi]
        for j in range(width - 1):
            back = width - 1 - j
            acc = acc + pltpu.roll(ext, back, 1)[:, pad:, :] * wdw_ref[j:j + 1, lo:hi]
        return (acc + bdw_ref[:, lo:hi]).reshape(gt * l, FF_CHUNK)

    n_cc = d_ff // FF_CHUNK
    ups = {}
    for cc in range(n_cc + 1):
        if cc < n_cc:
            ups[cc] = (up_proj(cc * FF_CHUNK), up_proj(d_ff + cc * FF_CHUNK))
        if cc >= 1:
            ug, uv = ups.pop(cc - 1)
            gate = conv(ug, (cc - 1) * FF_CHUNK)
            val = conv(uv, d_ff + (cc - 1) * FF_CHUNK)
            f_ref[:, (cc - 1) * FF_CHUNK:cc * FF_CHUNK] = (_silu(gate) * val).astype(jnp.bfloat16)

    ff = jnp.dot(f_ref[...], wdn_ref[...], preferred_element_type=jnp.float32)
    y_ref[...] = x1 + _rms(ff, gpo_ref[...])

    @pl.when(s == pl.num_programs(1) - 1)
    def _():
        st_ref[...] = carry_ref[:, h0:pad, :]


def _out_ffn(x, attn, c, hist, wo, gpm, gpf, wup, wdw, bdw, wdn, gpo, *, gt, l):
    g_total, s_total, d = x.shape
    n = g_total * s_total
    d_ff = wdn.shape[0]
    width = wdw.shape[0]
    pad = SUBLANES
    assert width - 1 <= pad <= l and d_ff % FF_CHUNK == 0
    rows = gt * l
    n_s = s_total // l
    assert s_total % l == 0 and g_total % gt == 0 and l % SUBLANES == 0
    assert gt == 1 or n_s == 1
    row_spec = lambda cdim: pl.BlockSpec((rows, cdim), lambda g, s: (g * n_s + s, 0))
    st_spec = pl.BlockSpec((gt, width - 1, 2 * d_ff), lambda g, s: (g, 0, 0))
    f32 = jnp.float32
    return pl.pallas_call(
        functools.partial(_out_ffn_kernel, gt=gt, l=l, width=width),
        grid=(g_total // gt, n_s),
        in_specs=[
            row_spec(d), row_spec(attn.shape[1]), row_spec(c.shape[1]), st_spec,
            _const_spec(wo.shape), _const_spec((1, d)), _const_spec((1, d)),
            _const_spec(wup.shape), _const_spec(wdw.shape), _const_spec((1, 2 * d_ff)),
            _const_spec(wdn.shape), _const_spec((1, d)),
        ],
        out_specs=[row_spec(d), st_spec],
        out_shape=[jax.ShapeDtypeStruct((n, d), f32),
                   jax.ShapeDtypeStruct((g_total, width - 1, 2 * d_ff), f32)],
        scratch_shapes=[pltpu.VMEM((gt, pad, 2 * d_ff), f32),
                        pltpu.VMEM((rows, d_ff), jnp.bfloat16)],
        compiler_params=pltpu.CompilerParams(
            dimension_semantics=("arbitrary", "arbitrary"), vmem_limit_bytes=VMEM_LIMIT),
        name="out_ffn",
    )(x.reshape(n, d), attn, c, hist, wo, gpm.reshape(1, d), gpf.reshape(1, d), wup, wdw,
      bdw.reshape(1, -1), wdn, gpo.reshape(1, d))


def kernel(x_prompt, x_sample, cache_k, cache_v, page_table, state_cv, state_ffn, g_pre_mix, w_in, b_in,
           w_cv_dw, b_cv_dw, g_cv_ln, b_cv_ln, w_cv_pw, b_cv_pw, w_out, g_post_mix, g_pre_ffn, w_up,
           w_ffn_dw, b_ffn_dw, w_down, g_post_ffn):
    depth = w_in.shape[0]
    bf16 = jnp.bfloat16
    b, s_len, d = x_prompt.shape
    db, t, _ = x_sample.shape
    n_qkv = 3 * D_ATTN
    kv_s = (db, t, N_HEADS, HEAD_DIM)
    yp, ys = x_prompt, x_sample
    outs = [[] for _ in range(8)]
    for lyr in range(depth):
        w_qkv = w_in[lyr][:, :n_qkv].astype(bf16)
        b_qkv = b_in[lyr][:n_qkv]
        cv_prm = (w_in[lyr][:, n_qkv:].astype(bf16), b_in[lyr][n_qkv:], w_cv_dw[lyr], b_cv_dw[lyr],
                  g_cv_ln[lyr], b_cv_ln[lyr], w_cv_pw[lyr].astype(bf16), b_cv_pw[lyr])
        ffn_prm = (w_out[lyr].astype(bf16), g_post_mix[lyr], g_pre_ffn[lyr], w_up[lyr].astype(bf16),
                   w_ffn_dw[lyr], b_ffn_dw[lyr], w_down[lyr].astype(bf16), g_post_ffn[lyr])

        q, k, v, c_s, cvs = _inproj_conv(ys, state_cv[lyr], g_pre_mix[lyr], w_qkv, b_qkv.reshape(1, n_qkv), *cv_prm,
                                         gt=db, l=t, transposed=False)

        cv0 = jnp.zeros((b,) + state_cv.shape[2:], x_prompt.dtype)
        ff0 = jnp.zeros((b,) + state_ffn.shape[2:], x_prompt.dtype)
        lp = min(ROW_TILE, s_len)
        qt, kt, vt, c, cvp, attn_s = _inproj_conv(
            yp, cv0, g_pre_mix[lyr], w_qkv.T, b_qkv.reshape(n_qkv, 1), *cv_prm, gt=1, l=lp, transposed=True,
            sample=(q, k, v, cache_k[lyr], cache_v[lyr], page_table, t))
        attn = _moba_prompt(qt, kt, vt).reshape(b * s_len, D_ATTN)
        yp, ffp = _out_ffn(yp, attn, c, ff0, *ffn_prm, gt=1, l=lp)
        yp = yp.reshape(b, s_len, d)
        kp = jnp.transpose(kt.reshape(b, N_HEADS, HEAD_DIM, s_len), (0, 3, 1, 2))
        vp = jnp.transpose(vt.reshape(b, N_HEADS, HEAD_DIM, s_len), (0, 3, 1, 2))

        ys, ffs = _out_ffn(ys, attn_s, c_s, state_ffn[lyr], *ffn_prm, gt=db, l=t)
        ys = ys.reshape(db, t, d)
        for lst, val in zip(outs, (kp, vp, cvp, ffp, k.reshape(kv_s), v.reshape(kv_s), cvs, ffs)):
            lst.append(val)
    return (yp, ys) + tuple(jnp.stack(lst) for lst in outs)
```
